```python
import math
import jax, jax.numpy as jnp
from jax import lax
import numpy as np

D_MODEL = 2048
BATCH = 2
SEQ = 4096
DEPTH = 4

D_MIX = D_MODEL
M_WIDTH = D_MIX // 2
M_HEADS = 4
M_HEAD_V = M_WIDTH // M_HEADS
M_HEAD_QK = M_HEAD_V // 2
M_CONV = 4
M_CHUNK = 64
S5_CH = D_MIX - M_WIDTH
S5_GROUP_CH = 16
S5_GROUPS = S5_CH // S5_GROUP_CH
S5_STATE = 64
D_FF = 5504
F_CONV = 3
EPS = 1e-6

C_XM = M_WIDTH
C_V = M_WIDTH
C_O = M_WIDTH
C_I = M_HEADS
C_F = M_HEADS
C_U = S5_CH
IN_COLS = C_XM + C_V + C_O + C_I + C_F + C_U
SPLITS = (C_XM, C_XM + C_V, C_XM + C_V + C_O, C_XM + C_V + C_O + C_I, C_XM + C_V + C_O + C_I + C_F)

kernel_name = "hybrid_mlstm_s5_convffn"

f32 = jnp.float32


def rmsnorm(x, g):
    xf = x.astype(f32)
    y = xf * lax.rsqrt(jnp.mean(xf * xf, axis=-1, keepdims=True) + EPS) * g.astype(f32)
    return y.astype(x.dtype)


def causal_dwconv(x, w, b):
    K, C = w.shape
    y = lax.conv_general_dilated(x, w[:, None, :].astype(x.dtype), window_strides=(1,), padding=[(K - 1, 0)], dimension_numbers=('NWC', 'WIO', 'NWC'), feature_group_count=C)
    return y + b.astype(x.dtype)


def mlstm_chunkwise(q, k, v, i_pre, f_pre):
    Bsz, H, T, DK = q.shape
    DV = v.shape[-1]
    L = M_CHUNK
    NC = T // L
    q = q * (DK ** -0.5)
    lf = jax.nn.log_sigmoid(f_pre)

    def to_chunks(a):
        return jnp.moveaxis(a.reshape(Bsz, H, NC, L, *a.shape[3:]), 2, 0)

    qc, kc, vc, ic, fc = (to_chunks(a) for a in (q, k, v, i_pre, lf))
    causal = jnp.tril(jnp.ones((L, L), dtype=bool))

    def step(carry, inp):
        C, n, m = carry
        qb, kb, vb, ib, fb = inp
        b = jnp.cumsum(fb, axis=-1)
        g = b[..., -1]
        Dm = jnp.where(causal, b[..., :, None] - b[..., None, :] + ib[..., None, :], -jnp.inf)
        inter = b + m[..., None]
        m_row = jnp.maximum(inter, jnp.max(Dm, axis=-1))
        w_intra = jnp.exp(Dm - m_row[..., None])
        w_inter = jnp.exp(inter - m_row)
        s = jnp.einsum('bhid,bhjd->bhij', qb, kb) * w_intra
        num = jnp.einsum('bhij,bhjv->bhiv', s, vb) + w_inter[..., None] * jnp.einsum('bhid,bhdv->bhiv', qb, C)
        den = jnp.sum(s, axis=-1) + w_inter * jnp.einsum('bhid,bhd->bhi', qb, n)
        h = num / jnp.maximum(jnp.abs(den), jnp.exp(-m_row))[..., None]
        dec = g[..., None] - b + ib
        m_new = jnp.maximum(g + m, jnp.max(dec, axis=-1))
        w_k = jnp.exp(dec - m_new[..., None])
        w_c = jnp.exp(g + m - m_new)
        C = w_c[..., None, None] * C + jnp.einsum('bhj,bhjd,bhjv->bhdv', w_k, kb, vb)
        n = w_c[..., None] * n + jnp.einsum('bhj,bhjd->bhd', w_k, kb)
        return (C, n, m_new), h

    init = (jnp.zeros((Bsz, H, DK, DV), f32), jnp.zeros((Bsz, H, DK), f32), jnp.zeros((Bsz, H), f32))
    _, hc = lax.scan(step, init, (qc, kc, vc, ic, fc))
    return jnp.moveaxis(hc, 0, 2).reshape(Bsz, H, T, DV)


def mlstm_mixer(xm, v_raw, o_raw, i_raw, f_raw, conv_w, conv_b, w_q, w_k, i_bias, f_bias, head_g, skip):
    Bsz, T, _ = xm.shape
    c = jax.nn.silu(causal_dwconv(xm, conv_w, conv_b)).astype(f32)
    ch = c.reshape(Bsz, T, M_HEADS, M_HEAD_V)
    q = jnp.einsum('bthd,hde->bhte', ch, w_q.astype(f32))
    k = jnp.einsum('bthd,hde->bhte', ch, w_k.astype(f32))
    v = v_raw.astype(f32).reshape(Bsz, T, M_HEADS, M_HEAD_V).transpose(0, 2, 1, 3)
    i_pre = (i_raw.astype(f32) + i_bias.astype(f32)).transpose(0, 2, 1)
    f_pre = (f_raw.astype(f32) + f_bias.astype(f32)).transpose(0, 2, 1)
    h = mlstm_chunkwise(q, k, v, i_pre, f_pre).transpose(0, 2, 1, 3)
    hn = h * lax.rsqrt(jnp.mean(h * h, axis=-1, keepdims=True) + EPS)
    hn = (hn * head_g.astype(f32).reshape(M_HEADS, M_HEAD_V)).reshape(Bsz, T, M_WIDTH)
    out = jax.nn.sigmoid(o_raw.astype(f32)) * (hn + skip.astype(f32) * c)
    return out.astype(xm.dtype)


def s5_mixer(u, a_re, a_im, log_dt, b_re, b_im, c_re, c_im, d, w_glu, b_glu):
    Bsz, T, _ = u.shape
    uf = u.astype(f32).reshape(Bsz, T, S5_GROUPS, S5_GROUP_CH)
    A = lax.complex(a_re.astype(f32), a_im.astype(f32))
    dt = jnp.exp(log_dt.astype(f32))[:, None]
    A_bar = jnp.exp(A * dt)
    Bm = lax.complex(b_re.astype(f32), b_im.astype(f32))
    B_bar = ((A_bar - 1.0) / A)[..., None] * Bm
    Cm = lax.complex(c_re.astype(f32), c_im.astype(f32))
    Bu = jnp.einsum('gpc,btgc->btgp', B_bar, uf.astype(B_bar.dtype))
    a = jnp.broadcast_to(A_bar, Bu.shape)

    def combine(e1, e2):
        a1, b1 = e1
        a2, b2 = e2
        return a2 * a1, a2 * b1 + b2

    _, states = lax.associative_scan(combine, (a, Bu), axis=1)
    y = jnp.real(jnp.einsum('gcp,btgp->btgc', Cm, states)) + d.astype(f32).reshape(S5_GROUPS, S5_GROUP_CH) * uf
    y = jax.nn.gelu(y.reshape(Bsz, T, S5_CH))
    out = y * jax.nn.sigmoid(y @ w_glu.astype(f32) + b_glu.astype(f32))
    return out.astype(u.dtype)


def setup_inputs(seed: int = 0) -> dict:
    key = jax.random.key(seed)
    ks = jax.random.split(key, 32)
    nrm = jax.random.normal
    L = DEPTH
    x = nrm(ks[0], (BATCH, SEQ, D_MODEL), f32)
    norm_mix_g = 1.0 + 0.02 * nrm(ks[1], (L, D_MODEL), f32)
    w_in = nrm(ks[2], (L, D_MODEL, IN_COLS), f32) * D_MODEL ** -0.5
    m_conv_w = nrm(ks[3], (L, M_CONV, M_WIDTH), f32) * M_CONV ** -0.5
    m_conv_b = 0.02 * nrm(ks[4], (L, M_WIDTH), f32)
    w_q = nrm(ks[5], (L, M_HEADS, M_HEAD_V, M_HEAD_QK), f32) * M_HEAD_V ** -0.5
    w_k = nrm(ks[6], (L, M_HEADS, M_HEAD_V, M_HEAD_QK), f32) * M_HEAD_V ** -0.5
    m_i_bias = 0.1 * nrm(ks[7], (L, M_HEADS), f32)
    m_f_bias = jnp.linspace(3.0, 6.0, M_HEADS, dtype=f32)[None, :] + 0.1 * nrm(ks[8], (L, M_HEADS), f32)
    m_head_g = 1.0 + 0.02 * nrm(ks[9], (L, M_WIDTH), f32)
    m_skip = 1.0 + 0.02 * nrm(ks[10], (L, M_WIDTH), f32)
    s5_a_re = -0.5 + 0.01 * nrm(ks[11], (L, S5_GROUPS, S5_STATE), f32)
    s5_a_im = math.pi * jnp.arange(S5_STATE, dtype=f32)[None, None, :] + 0.01 * nrm(ks[12], (L, S5_GROUPS, S5_STATE), f32)
    s5_log_dt = jax.random.uniform(ks[13], (L, S5_GROUPS), f32, math.log(1e-3), math.log(1e-1))
    s5_b_re = nrm(ks[14], (L, S5_GROUPS, S5_STATE, S5_GROUP_CH), f32) * (2 * S5_GROUP_CH) ** -0.5
    s5_b_im = nrm(ks[15], (L, S5_GROUPS, S5_STATE, S5_GROUP_CH), f32) * (2 * S5_GROUP_CH) ** -0.5
    s5_c_re = nrm(ks[16], (L, S5_GROUPS, S5_GROUP_CH, S5_STATE), f32) * (2 * S5_STATE) ** -0.5
    s5_c_im = nrm(ks[17], (L, S5_GROUPS, S5_GROUP_CH, S5_STATE), f32) * (2 * S5_STATE) ** -0.5
    s5_d = nrm(ks[18], (L, S5_CH), f32)
    s5_w_glu = nrm(ks[19], (L, S5_CH, S5_CH), f32) * S5_CH ** -0.5
    s5_b_glu = 0.02 * nrm(ks[20], (L, S5_CH), f32)
    w_out = nrm(ks[21], (L, D_MIX, D_MODEL), f32) * D_MIX ** -0.5
    norm_ffn_g = 1.0 + 0.02 * nrm(ks[22], (L, D_MODEL), f32)
    w_gate = nrm(ks[23], (L, D_MODEL, D_FF), f32) * D_MODEL ** -0.5
    w_val = nrm(ks[24], (L, D_MODEL, D_FF), f32) * D_MODEL ** -0.5
    f_conv_w = nrm(ks[25], (L, F_CONV, D_FF), f32) * F_CONV ** -0.5
    f_conv_b = 0.02 * nrm(ks[26], (L, D_FF), f32)
    w_down = nrm(ks[27], (L, D_FF, D_MODEL), f32) * D_FF ** -0.5
    norm_final_g = 1.0 + 0.02 * nrm(ks[28], (D_MODEL,), f32)
    return {"x": x, "norm_mix_g": norm_mix_g, "w_in": w_in, "m_conv_w": m_conv_w, "m_conv_b": m_conv_b, "w_q": w_q, "w_k": w_k, "m_i_bias": m_i_bias, "m_f_bias": m_f_bias, "m_head_g": m_head_g, "m_skip": m_skip, "s5_a_re": s5_a_re, "s5_a_im": s5_a_im, "s5_log_dt": s5_log_dt, "s5_b_re": s5_b_re, "s5_b_im": s5_b_im, "s5_c_re": s5_c_re, "s5_c_im": s5_c_im, "s5_d": s5_d, "s5_w_glu": s5_w_glu, "s5_b_glu": s5_b_glu, "w_out": w_out, "norm_ffn_g": norm_ffn_g, "w_gate": w_gate, "w_val": w_val, "f_conv_w": f_conv_w, "f_conv_b": f_conv_b, "w_down": w_down, "norm_final_g": norm_final_g}


def reference(x, norm_mix_g, w_in, m_conv_w, m_conv_b, w_q, w_k, m_i_bias, m_f_bias, m_head_g, m_skip, s5_a_re, s5_a_im, s5_log_dt, s5_b_re, s5_b_im, s5_c_re, s5_c_im, s5_d, s5_w_glu, s5_b_glu, w_out, norm_ffn_g, w_gate, w_val, f_conv_w, f_conv_b, w_down, norm_final_g):
    for l in range(DEPTH):
        h = rmsnorm(x, norm_mix_g[l])
        proj = h @ w_in[l].astype(h.dtype)
        xm, v_raw, o_raw, i_raw, f_raw, u = jnp.split(proj, SPLITS, axis=-1)
        m_out = mlstm_mixer(xm, v_raw, o_raw, i_raw, f_raw, m_conv_w[l], m_conv_b[l], w_q[l], w_k[l], m_i_bias[l], m_f_bias[l], m_head_g[l], m_skip[l])
        s_out = s5_mixer(u, s5_a_re[l], s5_a_im[l], s5_log_dt[l], s5_b_re[l], s5_b_im[l], s5_c_re[l], s5_c_im[l], s5_d[l], s5_w_glu[l], s5_b_glu[l])
        mix = jnp.concatenate([m_out, s_out], axis=-1)
        x = x + mix @ w_out[l].astype(mix.dtype)
        h = rmsnorm(x, norm_ffn_g[l])
        gate = causal_dwconv(h @ w_gate[l].astype(h.dtype), f_conv_w[l], f_conv_b[l])
        val = h @ w_val[l].astype(h.dtype)
        x = x + (jax.nn.gelu(gate) * val) @ w_down[l].astype(h.dtype)
    return rmsnorm(x, norm_final_g)
```

```python
import functools
import math

import jax
import jax.numpy as jnp
from jax import lax
from jax.experimental import pallas as pl
from jax.experimental.pallas import tpu as pltpu

F32 = jnp.float32
BF16 = jnp.bfloat16
EPS = 1e-6

M_HEADS = 4
M_HEAD_V = 256
M_HEAD_QK = 128
M_WIDTH = M_HEADS * M_HEAD_V
M_CONV = 4
M_CHUNK = 64
S5_GROUP_CH = 16
S5_STATE = 64
S5_CHUNK = 16
F_CONV = 3

LANES = 128
SUBLANES = 8
VMEM_LIMIT = 56 * 1024 * 1024


def _cparams(*sem):
    return pltpu.CompilerParams(dimension_semantics=sem, vmem_limit_bytes=VMEM_LIMIT)


def _rmsnorm_body(x_ref, g_ref, o_ref):
    x = x_ref[...]
    ms = jnp.mean(x * x, axis=-1, keepdims=True)
    o_ref[...] = (x * lax.rsqrt(ms + EPS) * g_ref[...]).astype(o_ref.dtype)


def _rmsnorm(x, g, out_dtype, tm=512):
    m, d = x.shape
    return pl.pallas_call(
        _rmsnorm_body,
        grid=(m // tm,),
        in_specs=[pl.BlockSpec((tm, d), lambda i: (i, 0)), pl.BlockSpec((1, d), lambda i: (0, 0))],
        out_specs=pl.BlockSpec((tm, d), lambda i: (i, 0)),
        out_shape=jax.ShapeDtypeStruct((m, d), out_dtype),
        compiler_params=_cparams("parallel"),
        name="rmsnorm",
    )(x, g.reshape(1, d))


def _matmul_body(a_ref, w_ref, o_ref):
    o_ref[...] = jnp.dot(a_ref[...], w_ref[...], preferred_element_type=F32).astype(o_ref.dtype)


def _matmul(a, w, out_dtype, tm, tn):
    m, k = a.shape
    n = w.shape[1]
    return pl.pallas_call(
        _matmul_body,
        grid=(m // tm, n // tn),
        in_specs=[pl.BlockSpec((tm, k), lambda i, j: (i, 0)), pl.BlockSpec((k, tn), lambda i, j: (0, j))],
        out_specs=pl.BlockSpec((tm, tn), lambda i, j: (i, j)),
        out_shape=jax.ShapeDtypeStruct((m, n), out_dtype),
        compiler_params=_cparams("parallel", "arbitrary"),
        name="in_proj",
    )(a, w)


def _log_sigmoid(x):
    return jnp.minimum(x, 0.0) - jnp.log1p(jnp.exp(-jnp.abs(x)))


def _mlstm_body(xm_ref, v_ref, o_ref, ig_ref, fg_ref, cw_ref, cb_ref, wq_ref, wk_ref, ib_ref, fb_ref,
                hg_ref, sk_ref, out_ref, ext_ref, c_ref, n_ref, m_ref, h_ref, *, tb):
    ncb = tb // M_CHUNK
    L = M_CHUNK

    @pl.when(pl.program_id(1) == 0)
    def _():
        ext_ref[0:SUBLANES, :] = jnp.zeros((SUBLANES, M_WIDTH), F32)
        c_ref[...] = jnp.zeros_like(c_ref)
        n_ref[...] = jnp.zeros_like(n_ref)
        m_ref[...] = jnp.zeros_like(m_ref)

    ext_ref[SUBLANES:, :] = xm_ref[...]
    acc = cb_ref[...] + cw_ref[0:1, :] * ext_ref[pl.ds(SUBLANES - 3, tb), :]
    for kk in range(1, M_CONV):
        acc = acc + cw_ref[kk:kk + 1, :] * ext_ref[pl.ds(SUBLANES - 3 + kk, tb), :]
    c = acc * jax.nn.sigmoid(acc)
    ext_ref[0:SUBLANES, :] = xm_ref[tb - SUBLANES:tb, :]

    i_pre = ig_ref[...] + ib_ref[...]
    lf = _log_sigmoid(fg_ref[...] + fb_ref[...])
    pos = lax.broadcasted_iota(jnp.int32, (tb, LANES), 0) % L
    bc = lf
    s = 1
    while s < L:
        bc = bc + jnp.where(pos >= s, pltpu.roll(bc, s, axis=0), 0.0)
        s *= 2
    rowv = i_pre - bc
    a = rowv
    s = 1
    while s < L:
        a = jnp.where(pos >= s, jnp.maximum(a, pltpu.roll(a, s, axis=0)), a)
        s *= 2
    bc3 = bc.reshape(ncb, L, LANES)
    g = bc3[:, L - 1:L, :]
    dec3 = g - bc3 + i_pre.reshape(ncb, L, LANES)
    maxdec = jnp.max(dec3, axis=1, keepdims=True)
    m_run = m_ref[0:1, :]
    m0_l, m1_l = [], []
    for nn in range(ncb):
        m0_l.append(m_run)
        m_run = jnp.maximum(g[nn] + m_run, maxdec[nn])
        m1_l.append(m_run)
    m_ref[0:1, :] = m_run
    m0 = jnp.stack(m0_l, axis=0)
    m1 = jnp.stack(m1_l, axis=0)
    mx = jnp.maximum(m0, a.reshape(ncb, L, LANES))
    colv = -mx
    w_inter = jnp.exp(m0 - mx)
    e_negm = jnp.exp(-(bc3 + mx))
    w_k = jnp.exp(dec3 - m1)
    w_c = jnp.exp(g + m0 - m1)
    rowv_t = rowv.T

    ii = lax.broadcasted_iota(jnp.int32, (L, 2 * L), 0)
    jj = lax.broadcasted_iota(jnp.int32, (L, 2 * L), 1)
    masks = [(jj <= ii), (jj >= L) & (jj - L <= ii)]
    scale = M_HEAD_QK ** -0.5

    for h in range(M_HEADS):
        c_h = c[:, h * M_HEAD_V:(h + 1) * M_HEAD_V].astype(BF16)
        q = jnp.dot(c_h, wq_ref[h], preferred_element_type=F32) * scale
        k = jnp.dot(c_h, wk_ref[h], preferred_element_type=F32)
        qb = q.astype(BF16)
        kb = k.astype(BF16)
        vb = v_ref[:, h * M_HEAD_V:(h + 1) * M_HEAD_V].astype(BF16)
        c_st = c_ref[h]
        n_st = n_ref[h]
        for nn in range(ncb):
            p0 = (nn // 2) * 2 * L
            r0 = nn * L
            q_n = q[r0:r0 + L]
            qb_n = qb[r0:r0 + L]
            sc = lax.dot_general(qb_n, kb[p0:p0 + 2 * L], (((1,), (1,)), ((), ())),
                                 preferred_element_type=F32)
            arg = colv[nn][:, h:h + 1] + rowv_t[h:h + 1, p0:p0 + 2 * L]
            sw = sc * jnp.exp(jnp.where(masks[nn % 2], arg, -jnp.inf))
            wi = w_inter[nn][:, h:h + 1]
            num = jnp.dot(sw.astype(BF16), vb[p0:p0 + 2 * L], preferred_element_type=F32)
            num = num + wi * jnp.dot(qb_n, c_st.astype(BF16), preferred_element_type=F32)
            den = jnp.sum(sw, axis=1, keepdims=True) + wi * jnp.sum(q_n * n_st, axis=1, keepdims=True)
            den = jnp.maximum(jnp.abs(den), e_negm[nn][:, h:h + 1])
            h_ref[r0:r0 + L, h * M_HEAD_V:(h + 1) * M_HEAD_V] = num / den
            kw = k[r0:r0 + L] * w_k[nn][:, h:h + 1]
            wc = w_c[nn][:, h:h + 1]
            upd = lax.dot_general(kw.astype(BF16), vb[r0:r0 + L], (((0,), (0,)), ((), ())),
                                  preferred_element_type=F32)
            c_st = wc * c_st + upd
            n_st = wc * n_st + jnp.sum(kw, axis=0, keepdims=True)
        c_ref[h] = c_st
        n_ref[h] = n_st

    for h in range(M_HEADS):
        sl = slice(h * M_HEAD_V, (h + 1) * M_HEAD_V)
        hh = h_ref[:, sl]
        hn = hh * lax.rsqrt(jnp.mean(hh * hh, axis=-1, keepdims=True) + EPS) * hg_ref[:, sl]
        out = jax.nn.sigmoid(o_ref[:, sl]) * (hn + sk_ref[:, sl] * c[:, sl])
        out_ref[:, sl] = out.astype(out_ref.dtype)


def _mlstm(proj, cw, cb, wq, wk, ib, fb, hg, sk, bsz, t, tb=256):
    nt = t // tb
    gate_blk = (3 * M_WIDTH + proj.shape[1] - 3 * M_WIDTH - 2 * LANES) // LANES
    row = lambda b, i: b * nt + i
    full = lambda shape: pl.BlockSpec(shape, lambda b, i: (0,) * len(shape))
    return pl.pallas_call(
        functools.partial(_mlstm_body, tb=tb),
        grid=(bsz, nt),
        in_specs=[
            pl.BlockSpec((tb, M_WIDTH), lambda b, i: (row(b, i), 0)),
            pl.BlockSpec((tb, M_WIDTH), lambda b, i: (row(b, i), 1)),
            pl.BlockSpec((tb, M_WIDTH), lambda b, i: (row(b, i), 2)),
            pl.BlockSpec((tb, LANES), lambda b, i: (row(b, i), gate_blk)),
            pl.BlockSpec((tb, LANES), lambda b, i: (row(b, i), gate_blk + 1)),
            full((M_CONV, M_WIDTH)), full((1, M_WIDTH)),
            full((M_HEADS, M_HEAD_V, M_HEAD_QK)), full((M_HEADS, M_HEAD_V, M_HEAD_QK)),
            full((1, LANES)), full((1, LANES)), full((1, M_WIDTH)), full((1, M_WIDTH)),
        ],
        out_specs=pl.BlockSpec((tb, M_WIDTH), lambda b, i: (row(b, i), 0)),
        out_shape=jax.ShapeDtypeStruct((bsz * t, M_WIDTH), BF16),
        scratch_shapes=[
            pltpu.VMEM((tb + SUBLANES, M_WIDTH), F32),
            pltpu.VMEM((M_HEADS, M_HEAD_QK, M_HEAD_V), F32),
            pltpu.VMEM((M_HEADS, 1, M_HEAD_QK), F32),
            pltpu.VMEM((SUBLANES, LANES), F32),
            pltpu.VMEM((tb, M_WIDTH), F32),
        ],
        compiler_params=_cparams("parallel", "arbitrary"),
        name="mlstm",
    )(proj, proj, proj, proj, proj, cw, cb, wq, wk, ib, fb, hg, sk)


def _s5_body(u_ref, tz_ref, wsr_ref, wsi_ref, wcr_ref, wci_ref, ar_ref, ai_ref, br_ref, bi_ref,
             pr_ref, pi_ref, y_ref, vr_ref, vi_ref, sr_ref, si_ref, *, nl, nseq, bsz):
    u0 = u_ref[0]
    u1 = u_ref[1]
    y0 = jnp.dot(u0, tz_ref[0], preferred_element_type=F32)
    y1 = jnp.dot(u1, tz_ref[1], preferred_element_type=F32)
    ucat = jnp.concatenate([u0, u1], axis=1)
    vr_ref[...] = jnp.dot(ucat, wsr_ref[0], preferred_element_type=F32)
    vi_ref[...] = jnp.dot(ucat, wsi_ref[0], preferred_element_type=F32)

    ar = jnp.broadcast_to(ar_ref[0], (nseq, LANES))
    ai = jnp.broadcast_to(ai_ref[0], (nseq, LANES))

    def scan_step(i, carry):
        s_re, s_im = carry
        r0 = pl.multiple_of(i * nseq, nseq)
        sr_ref[pl.ds(r0, nseq), :] = s_re
        si_ref[pl.ds(r0, nseq), :] = s_im
        n_re = ar * s_re - ai * s_im + vr_ref[pl.ds(r0, nseq), :]
        n_im = ar * s_im + ai * s_re + vi_ref[pl.ds(r0, nseq), :]
        return n_re, n_im

    zero = jnp.zeros((nseq, LANES), F32)
    f_re, f_im = lax.fori_loop(0, nl, scan_step, (zero, zero))

    br = jnp.broadcast_to(br_ref[0], (nseq, LANES))
    bi = jnp.broadcast_to(bi_ref[0], (nseq, LANES))
    rows = lax.broadcasted_iota(jnp.int32, (nseq, LANES), 0)
    i_re, i_im = zero, zero
    for _ in range(nseq // bsz - 1):
        t_re = f_re + br * i_re - bi * i_im
        t_im = f_im + br * i_im + bi * i_re
        i_re = jnp.where(rows >= bsz, pltpu.roll(t_re, bsz, axis=0), 0.0)
        i_im = jnp.where(rows >= bsz, pltpu.roll(t_im, bsz, axis=0), 0.0)

    def fix_step(i, _):
        r0 = pl.multiple_of(i * nseq, nseq)
        p_re = pr_ref[0, pl.ds(i, 1), :]
        p_im = pi_ref[0, pl.ds(i, 1), :]
        sr_ref[pl.ds(r0, nseq), :] += p_re * i_re - p_im * i_im
        si_ref[pl.ds(r0, nseq), :] += p_re * i_im + p_im * i_re
        return 0

    lax.fori_loop(0, nl, fix_step, 0)

    y = jnp.dot(sr_ref[...].astype(BF16), wcr_ref[0], preferred_element_type=F32)
    y = y + jnp.dot(si_ref[...].astype(BF16), wci_ref[0], preferred_element_type=F32)
    y_ref[0] = y + jnp.concatenate([y0, y1], axis=1)


def _s5(ug, prm, bsz, nl, nseq):
    g, r, pw = ug.shape
    g2 = g // 2
    blk = lambda shape: pl.BlockSpec((1,) + shape, lambda i: (i,) + (0,) * len(shape))
    return pl.pallas_call(
        functools.partial(_s5_body, nl=nl, nseq=nseq, bsz=bsz),
        grid=(g2,),
        in_specs=[
            pl.BlockSpec((2, r, pw), lambda i: (i, 0, 0)),
            pl.BlockSpec((2, pw, pw), lambda i: (i, 0, 0)),
            blk((2 * pw, LANES)), blk((2 * pw, LANES)),
            blk((LANES, 2 * pw)), blk((LANES, 2 * pw)),
            blk((1, LANES)), blk((1, LANES)), blk((1, LANES)), blk((1, LANES)),
            blk((nl, LANES)), blk((nl, LANES)),
        ],
        out_specs=blk((r, 2 * pw)),
        out_shape=jax.ShapeDtypeStruct((g2, r, 2 * pw), F32),
        scratch_shapes=[pltpu.VMEM((r, LANES), F32) for _ in range(4)],
        compiler_params=_cparams("parallel"),
        name="s5",
    )(ug, prm["tz"], prm["ws_re"], prm["ws_im"], prm["wc_re"], prm["wc_im"],
      prm["a_re"], prm["a_im"], prm["b_re"], prm["b_im"], prm["p_re"], prm["p_im"])


def _s5_params(a_re, a_im, log_dt, b_re, b_im, c_re, c_im, nl):
    hp = lax.Precision.HIGHEST
    lc = S5_CHUNK
    dt = jnp.exp(log_dt)[..., None]
    lam_re, lam_im = a_re * dt, a_im * dt

    def apow(n):
        n = n.astype(F32)[:, None, None, None]
        mag = jnp.exp(lam_re[None] * n)
        return mag * jnp.cos(lam_im[None] * n), mag * jnp.sin(lam_im[None] * n)

    ab_re, ab_im = jnp.exp(lam_re) * jnp.cos(lam_im), jnp.exp(lam_re) * jnp.sin(lam_im)
    den = a_re * a_re + a_im * a_im
    f_re = ((ab_re - 1.0) * a_re + ab_im * a_im) / den
    f_im = (ab_im * a_re - (ab_re - 1.0) * a_im) / den
    bb_re = f_re[..., None] * b_re - f_im[..., None] * b_im
    bb_im = f_re[..., None] * b_im + f_im[..., None] * b_re
    pw_re, pw_im = apow(jnp.arange(lc + 1))
    cp_re = jnp.einsum('lgcp,tlgp->tlgcp', c_re, pw_re[:lc]) - jnp.einsum('lgcp,tlgp->tlgcp', c_im, pw_im[:lc])
    cp_im = jnp.einsum('lgcp,tlgp->tlgcp', c_re, pw_im[:lc]) + jnp.einsum('lgcp,tlgp->tlgcp', c_im, pw_re[:lc])
    kk = (jnp.einsum('tlgcp,lgpd->tlgcd', cp_re, bb_re, precision=hp)
          - jnp.einsum('tlgcp,lgpd->tlgcd', cp_im, bb_im, precision=hp))
    idx = jnp.arange(lc)
    tau = idx[None, :] - idx[:, None]
    tz = jnp.where((tau >= 0)[:, :, None, None, None, None], kk[jnp.clip(tau, 0, lc - 1)], 0.0)
    tz = tz.transpose(2, 3, 0, 5, 1, 4)
    nlay, ng = a_re.shape[0], a_re.shape[1]
    pw = lc * S5_GROUP_CH
    tz = tz.reshape(nlay, ng, pw, pw).astype(BF16)
    rp_re, rp_im = pw_re[:lc][::-1], pw_im[:lc][::-1]
    ws_re = jnp.einsum('ilgp,lgpd->lgidp', rp_re, bb_re) - jnp.einsum('ilgp,lgpd->lgidp', rp_im, bb_im)
    ws_im = jnp.einsum('ilgp,lgpd->lgidp', rp_re, bb_im) + jnp.einsum('ilgp,lgpd->lgidp', rp_im, bb_re)
    q_re, q_im = pw_re[1:lc + 1], pw_im[1:lc + 1]
    wc_re = jnp.einsum('lgcp,jlgp->lgpjc', c_re, q_re) - jnp.einsum('lgcp,jlgp->lgpjc', c_im, q_im)
    wc_im = -(jnp.einsum('lgcp,jlgp->lgpjc', c_re, q_im) + jnp.einsum('lgcp,jlgp->lgpjc', c_im, q_re))
    eye2 = jnp.eye(2, dtype=F32)

    def pair_rows(w):
        w = w.reshape(nlay, ng // 2, 2, pw, S5_STATE)
        return (w[:, :, :, :, None, :] * eye2[None, None, :, None, :, None]).reshape(nlay, ng // 2, 2 * pw, 2 * S5_STATE)

    def pair_cols(w):
        w = w.reshape(nlay, ng // 2, 2, S5_STATE, pw)
        return (w[:, :, :, :, None, :] * eye2[None, None, :, None, :, None]).reshape(nlay, ng // 2, 2 * S5_STATE, 2 * pw)

    def lanes(x):
        return x.reshape(x.shape[:-2] + (ng // 2, 2 * S5_STATE))

    ac_re, ac_im = apow(jnp.array([lc]))
    as_re, as_im = apow(jnp.array([lc * nl]))
    pt_re, pt_im = apow(jnp.arange(nl) * lc)
    return {
        "tz": tz,
        "ws_re": pair_rows(ws_re.reshape(nlay, ng, pw, S5_STATE)).astype(BF16),
        "ws_im": pair_rows(ws_im.reshape(nlay, ng, pw, S5_STATE)).astype(BF16),
        "wc_re": pair_cols(wc_re.reshape(nlay, ng, S5_STATE, pw)).astype(BF16),
        "wc_im": pair_cols(wc_im.reshape(nlay, ng, S5_STATE, pw)).astype(BF16),
        "a_re": lanes(ac_re[0])[:, :, None, :], "a_im": lanes(ac_im[0])[:, :, None, :],
        "b_re": lanes(as_re[0])[:, :, None, :], "b_im": lanes(as_im[0])[:, :, None, :],
        "p_re": lanes(pt_re).transpose(1, 2, 0, 3), "p_im": lanes(pt_im).transpose(1, 2, 0, 3),
    }


def _glu_body(y_ref, u_ref, d_ref, w_ref, b_ref, o_ref):
    y = jax.nn.gelu(y_ref[...] + d_ref[...] * u_ref[...])
    z = jnp.dot(y.astype(BF16), w_ref[...], preferred_element_type=F32) + b_ref[...]
    o_ref[...] = (y * jax.nn.sigmoid(z)).astype(o_ref.dtype)


def _glu(y, proj, d, w, b, tm=512):
    m, n = y.shape
    u_blk = 3 * M_WIDTH // n
    return pl.pallas_call(
        _glu_body,
        grid=(m // tm,),
        in_specs=[
            pl.BlockSpec((tm, n), lambda i: (i, 0)),
            pl.BlockSpec((tm, n), lambda i: (i, u_blk)),
            pl.BlockSpec((1, n), lambda i: (0, 0)),
            pl.BlockSpec((n, n), lambda i: (0, 0)),
            pl.BlockSpec((1, n), lambda i: (0, 0)),
        ],
        out_specs=pl.BlockSpec((tm, n), lambda i: (i, 0)),
        out_shape=jax.ShapeDtypeStruct((m, n), BF16),
        compiler_params=_cparams("parallel"),
        name="s5_glu",
    )(y, proj, d, w, b)


def _outproj_body(m_ref, s_ref, wm_ref, ws_ref, x_ref, g_ref, xo_ref, ho_ref):
    acc = jnp.dot(m_ref[...], wm_ref[...], preferred_element_type=F32)
    acc = acc + jnp.dot(s_ref[...], ws_ref[...], preferred_element_type=F32)
    x = x_ref[...] + acc
    xo_ref[...] = x
    ms = jnp.mean(x * x, axis=-1, keepdims=True)
    ho_ref[...] = (x * lax.rsqrt(ms + EPS) * g_ref[...]).astype(ho_ref.dtype)


def _outproj(m_out, s_out, w, x, g, tm=256):
    m, d = x.shape
    kh = m_out.shape[1]
    return pl.pallas_call(
        _outproj_body,
        grid=(m // tm,),
        in_specs=[
            pl.BlockSpec((tm, kh), lambda i: (i, 0)),
            pl.BlockSpec((tm, kh), lambda i: (i, 0)),
            pl.BlockSpec((kh, d), lambda i: (0, 0)),
            pl.BlockSpec((kh, d), lambda i: (1, 0)),
            pl.BlockSpec((tm, d), lambda i: (i, 0)),
            pl.BlockSpec((1, d), lambda i: (0, 0)),
        ],
        out_specs=[pl.BlockSpec((tm, d), lambda i: (i, 0)), pl.BlockSpec((tm, d), lambda i: (i, 0))],
        out_shape=[jax.ShapeDtypeStruct((m, d), F32), jax.ShapeDtypeStruct((m, d), BF16)],
        compiler_params=_cparams("parallel"),
        name="out_proj",
    )(m_out, s_out, w, w, x, g)


def _ffn_up_body(h_ref, wg_ref, wv_ref, cw_ref, cb_ref, a_ref, ext_ref, *, tm, tiles_per_seq):
    g = jnp.dot(h_ref[...], wg_ref[...], preferred_element_type=F32)

    @pl.when(pl.program_id(1) % tiles_per_seq == 0)
    def _():
        ext_ref[0:SUBLANES, :] = jnp.zeros((SUBLANES, g.shape[1]), F32)

    ext_ref[SUBLANES:, :] = g
    conv = cb_ref[...] + cw_ref[F_CONV - 1:F_CONV, :] * g
    for kk in range(F_CONV - 1):
        conv = conv + cw_ref[kk:kk + 1, :] * ext_ref[pl.ds(SUBLANES - (F_CONV - 1) + kk, tm), :]
    ext_ref[0:SUBLANES, :] = g[tm - SUBLANES:tm, :]
    val = jnp.dot(h_ref[...], wv_ref[...], preferred_element_type=F32)
    a_ref[...] = (jax.nn.gelu(conv) * val).astype(a_ref.dtype)


def _ffn_up(h, wg, wv, cw, cb, t, tm=1024, tn=512):
    m, k = h.shape
    n = wg.shape[1]
    return pl.pallas_call(
        functools.partial(_ffn_up_body, tm=tm, tiles_per_seq=t // tm),
        grid=(n // tn, m // tm),
        in_specs=[
            pl.BlockSpec((tm, k), lambda j, i: (i, 0)),
            pl.BlockSpec((k, tn), lambda j, i: (0, j)),
            pl.BlockSpec((k, tn), lambda j, i: (0, j)),
            pl.BlockSpec((F_CONV, tn), lambda j, i: (0, j)),
            pl.BlockSpec((1, tn), lambda j, i: (0, j)),
        ],
        out_specs=pl.BlockSpec((tm, tn), lambda j, i: (i, j)),
        out_shape=jax.ShapeDtypeStruct((m, n), BF16),
        scratch_shapes=[pltpu.VMEM((tm + SUBLANES, tn), F32)],
        compiler_params=_cparams("parallel", "arbitrary"),
        name="ffn_up",
    )(h, wg, wv, cw, cb)


def _ffn_down_body(a_ref, w_ref, x_ref, g_ref, xo_ref, ho_ref, acc_ref):
    kk = pl.program_id(1)

    @pl.when(kk == 0)
    def _():
        acc_ref[...] = x_ref[...]

    acc_ref[...] += jnp.dot(a_ref[...], w_ref[...], preferred_element_type=F32)

    @pl.when(kk == pl.num_programs(1) - 1)
    def _():
        x = acc_ref[...]
        xo_ref[...] = x
        ms = jnp.mean(x * x, axis=-1, keepdims=True)
        ho_ref[...] = (x * lax.rsqrt(ms + EPS) * g_ref[...]).astype(ho_ref.dtype)


def _ffn_down(a, w, x, g, h_dtype, tm=512, tk=512):
    m, d = x.shape
    k = a.shape[1]
    return pl.pallas_call(
        _ffn_down_body,
        grid=(m // tm, k // tk),
        in_specs=[
            pl.BlockSpec((tm, tk), lambda i, j: (i, j)),
            pl.BlockSpec((tk, d), lambda i, j: (j, 0)),
            pl.BlockSpec((tm, d), lambda i, j: (i, 0)),
            pl.BlockSpec((1, d), lambda i, j: (0, 0)),
        ],
        out_specs=[pl.BlockSpec((tm, d), lambda i, j: (i, 0)), pl.BlockSpec((tm, d), lambda i, j: (i, 0))],
        out_shape=[jax.ShapeDtypeStruct((m, d), F32), jax.ShapeDtypeStruct((m, d), h_dtype)],
        scratch_shapes=[pltpu.VMEM((tm, d), F32)],
        compiler_params=_cparams("parallel", "arbitrary"),
        name="ffn_down",
    )(a, w, x, g)


def _pad_cols(w, mult):
    pad = (-w.shape[-1]) % mult
    return jnp.pad(w, [(0, 0)] * (w.ndim - 1) + [(0, pad)])


def _trunk(x, norm_mix_g, w_in, m_conv_w, m_conv_b, w_q, w_k, m_i_bias, m_f_bias, m_head_g, m_skip,
           s5_a_re, s5_a_im, s5_log_dt, s5_b_re, s5_b_im, s5_c_re, s5_c_im, s5_d, s5_w_glu, s5_b_glu,
           w_out, norm_ffn_g, w_gate, w_val, f_conv_w, f_conv_b, w_down, norm_final_g):
    bsz, t, d = x.shape
    depth = w_in.shape[0]
    bt = bsz * t
    s5_ch = s5_d.shape[1]
    ng = s5_ch // S5_GROUP_CH
    nseq = SUBLANES
    nsg = nseq // bsz
    nl = t // (nsg * S5_CHUNK)
    d_ff = w_gate.shape[2]

    c3 = 3 * M_WIDTH
    w_in_p = jnp.concatenate([
        w_in[:, :, :c3], w_in[:, :, c3 + 2 * M_HEADS:],
        _pad_cols(w_in[:, :, c3:c3 + M_HEADS], LANES), _pad_cols(w_in[:, :, c3 + M_HEADS:c3 + 2 * M_HEADS], LANES),
    ], axis=-1).astype(BF16)
    ib = _pad_cols(m_i_bias, LANES)[:, None, :]
    fb = _pad_cols(m_f_bias, LANES)[:, None, :]
    wq_b, wk_b = w_q.astype(BF16), w_k.astype(BF16)
    s5p = _s5_params(s5_a_re, s5_a_im, s5_log_dt, s5_b_re, s5_b_im, s5_c_re, s5_c_im, nl)
    wglu_b = s5_w_glu.astype(BF16)
    wout_b = w_out.astype(BF16)
    ff_mult = 512
    wg_b = _pad_cols(w_gate, ff_mult).astype(BF16)
    wv_b = _pad_cols(w_val, ff_mult).astype(BF16)
    fcw = _pad_cols(f_conv_w, ff_mult)
    fcb = _pad_cols(f_conv_b, ff_mult)[:, None, :]
    d_ffp = wg_b.shape[2]
    wd_b = jnp.pad(w_down, ((0, 0), (0, d_ffp - d_ff), (0, 0))).astype(BF16)

    xf = x.reshape(bt, d)
    h = _rmsnorm(xf, norm_mix_g[0], BF16)
    out = None
    for l in range(depth):
        proj = _matmul(h, w_in_p[l], F32, tm=1024, tn=256)
        m_out = _mlstm(proj, m_conv_w[l], m_conv_b[l][None], wq_b[l], wk_b[l], ib[l], fb[l],
                       m_head_g[l][None], m_skip[l][None], bsz, t)
        u = proj[:, c3:c3 + s5_ch].astype(BF16).reshape(bsz, nsg, nl, S5_CHUNK, ng, S5_GROUP_CH)
        ug = u.transpose(4, 2, 1, 0, 3, 5).reshape(ng, nl * nseq, S5_CHUNK * S5_GROUP_CH)
        yg = _s5(ug, {k_: v_[l] for k_, v_ in s5p.items()}, bsz, nl, nseq)
        y = yg.reshape(ng // 2, nl, nsg, bsz, 2, S5_CHUNK, S5_GROUP_CH).transpose(3, 2, 1, 5, 0, 4, 6).reshape(bt, s5_ch)
        s_out = _glu(y, proj, s5_d[l][None], wglu_b[l], s5_b_glu[l][None])
        xf, h = _outproj(m_out, s_out, wout_b[l], xf, norm_ffn_g[l][None])
        a = _ffn_up(h, wg_b[l], wv_b[l], fcw[l], fcb[l], t)
        last = l == depth - 1
        g_next = norm_final_g if last else norm_mix_g[l + 1]
        xf, h = _ffn_down(a, wd_b[l], xf, g_next[None], F32 if last else BF16)
        out = h
    return out.reshape(bsz, t, d)


def kernel(x, norm_mix_g, w_in, m_conv_w, m_conv_b, w_q, w_k, m_i_bias, m_f_bias, m_head_g, m_skip, s5_a_re, s5_a_im, s5_log_dt, s5_b_re, s5_b_im, s5_c_re, s5_c_im, s5_d, s5_w_glu, s5_b_glu, w_out, norm_ffn_g, w_gate, w_val, f_conv_w, f_conv_b, w_down, norm_final_g):
    return _trunk(x, norm_mix_g, w_in, m_conv_w, m_conv_b, w_q, w_k, m_i_bias, m_f_bias, m_head_g, m_skip,
                  s5_a_re, s5_a_im, s5_log_dt, s5_b_re, s5_b_im, s5_c_re, s5_c_im, s5_d, s5_w_glu, s5_b_glu,
                  w_out, norm_ffn_g, w_gate, w_val, f_conv_w, f_conv_b, w_down, norm_final_g)
```

```python
import functools
import math

import jax
import jax.numpy as jnp
from jax import lax
from jax.experimental import pallas as pl
from jax.experimental.pallas import tpu as pltpu

F32 = jnp.float32
BF16 = jnp.bfloat16
EPS = 1e-6

M_HEADS = 4
M_HEAD_V = 256
M_HEAD_QK = 128
M_WIDTH = M_HEADS * M_HEAD_V
M_CONV = 4
M_CHUNK = 64
S5_GROUP_CH = 16
S5_STATE = 64
S5_CHUNK = 8
F_CONV = 3

LANES = 128
SUBLANES = 8
S5_SLAB_GROUPS = LANES // S5_GROUP_CH
S5_SLAB_STATE = S5_SLAB_GROUPS * S5_STATE
VMEM_LIMIT = 56 * 1024 * 1024


def _cparams(*sem):
    return pltpu.CompilerParams(dimension_semantics=sem, vmem_limit_bytes=VMEM_LIMIT)


def _rmsnorm_body(x_ref, g_ref, o_ref):
    x = x_ref[...]
    ms = jnp.mean(x * x, axis=-1, keepdims=True)
    o_ref[...] = (x * lax.rsqrt(ms + EPS) * g_ref[...]).astype(o_ref.dtype)


def _rmsnorm(x, g, out_dtype, tm=512):
    m, d = x.shape
    return pl.pallas_call(
        _rmsnorm_body,
        grid=(m // tm,),
        in_specs=[pl.BlockSpec((tm, d), lambda i: (i, 0)), pl.BlockSpec((1, d), lambda i: (0, 0))],
        out_specs=pl.BlockSpec((tm, d), lambda i: (i, 0)),
        out_shape=jax.ShapeDtypeStruct((m, d), out_dtype),
        compiler_params=_cparams("parallel"),
        name="rmsnorm",
    )(x, g.reshape(1, d))


def _matmul_body(a_ref, w_ref, o_ref):
    o_ref[...] = jnp.dot(a_ref[...], w_ref[...], preferred_element_type=F32).astype(o_ref.dtype)


def _matmul(a, w, out_dtype, tm, tn):
    m, k = a.shape
    n = w.shape[1]
    return pl.pallas_call(
        _matmul_body,
        grid=(m // tm, n // tn),
        in_specs=[pl.BlockSpec((tm, k), lambda i, j: (i, 0)), pl.BlockSpec((k, tn), lambda i, j: (0, j))],
        out_specs=pl.BlockSpec((tm, tn), lambda i, j: (i, j)),
        out_shape=jax.ShapeDtypeStruct((m, n), out_dtype),
        compiler_params=_cparams("parallel", "arbitrary"),
        name="in_proj",
    )(a, w)


def _log_sigmoid(x):
    return jnp.minimum(x, 0.0) - jnp.log1p(jnp.exp(-jnp.abs(x)))


def _mlstm_body(xm_ref, v_ref, o_ref, ig_ref, fg_ref, cw_ref, cb_ref, wq_ref, wk_ref, ib_ref, fb_ref,
                hg_ref, sk_ref, out_ref, ext_ref, c_ref, n_ref, m_ref, h_ref, *, tb):
    ncb = tb // M_CHUNK
    L = M_CHUNK

    @pl.when(pl.program_id(1) == 0)
    def _():
        ext_ref[0:SUBLANES, :] = jnp.zeros((SUBLANES, M_WIDTH), F32)
        c_ref[...] = jnp.zeros_like(c_ref)
        n_ref[...] = jnp.zeros_like(n_ref)
        m_ref[...] = jnp.zeros_like(m_ref)

    ext_ref[SUBLANES:, :] = xm_ref[...]
    acc = cb_ref[...] + cw_ref[0:1, :] * ext_ref[pl.ds(SUBLANES - 3, tb), :]
    for kk in range(1, M_CONV):
        acc = acc + cw_ref[kk:kk + 1, :] * ext_ref[pl.ds(SUBLANES - 3 + kk, tb), :]
    c = acc * jax.nn.sigmoid(acc)
    ext_ref[0:SUBLANES, :] = xm_ref[tb - SUBLANES:tb, :]

    i_pre = ig_ref[...] + ib_ref[...]
    lf = _log_sigmoid(fg_ref[...] + fb_ref[...])
    pos = lax.broadcasted_iota(jnp.int32, (tb, LANES), 0) % L
    bc = lf
    s = 1
    while s < L:
        bc = bc + jnp.where(pos >= s, pltpu.roll(bc, s, axis=0), 0.0)
        s *= 2
    rowv = i_pre - bc
    a = rowv
    s = 1
    while s < L:
        a = jnp.where(pos >= s, jnp.maximum(a, pltpu.roll(a, s, axis=0)), a)
        s *= 2
    bc3 = bc.reshape(ncb, L, LANES)
    g = bc3[:, L - 1:L, :]
    dec3 = g - bc3 + i_pre.reshape(ncb, L, LANES)
    maxdec = jnp.max(dec3, axis=1, keepdims=True)
    m_run = m_ref[0:1, :]
    m0_l, m1_l = [], []
    for nn in range(ncb):
        m0_l.append(m_run)
        m_run = jnp.maximum(g[nn] + m_run, maxdec[nn])
        m1_l.append(m_run)
    m_ref[0:1, :] = m_run
    m0 = jnp.stack(m0_l, axis=0)
    m1 = jnp.stack(m1_l, axis=0)
    mx = jnp.maximum(m0, a.reshape(ncb, L, LANES))
    colv = -mx
    w_inter = jnp.exp(m0 - mx)
    e_negm = jnp.exp(-(bc3 + mx))
    w_k = jnp.exp(dec3 - m1)
    w_c = jnp.exp(g + m0 - m1)
    rowv_t = rowv.T

    ii = lax.broadcasted_iota(jnp.int32, (L, 2 * L), 0)
    jj = lax.broadcasted_iota(jnp.int32, (L, 2 * L), 1)
    masks = [(jj <= ii), (jj >= L) & (jj - L <= ii)]
    scale = M_HEAD_QK ** -0.5

    for h in range(M_HEADS):
        c_h = c[:, h * M_HEAD_V:(h + 1) * M_HEAD_V].astype(BF16)
        q = jnp.dot(c_h, wq_ref[h], preferred_element_type=F32) * scale
        k = jnp.dot(c_h, wk_ref[h], preferred_element_type=F32)
        qb = q.astype(BF16)
        kb = k.astype(BF16)
        vb = v_ref[:, h * M_HEAD_V:(h + 1) * M_HEAD_V].astype(BF16)
        c_st = c_ref[h]
        n_st = n_ref[h]
        for nn in range(ncb):
            p0 = (nn // 2) * 2 * L
            r0 = nn * L
            q_n = q[r0:r0 + L]
            qb_n = qb[r0:r0 + L]
            sc = lax.dot_general(qb_n, kb[p0:p0 + 2 * L], (((1,), (1,)), ((), ())),
                                 preferred_element_type=F32)
            arg = colv[nn][:, h:h + 1] + rowv_t[h:h + 1, p0:p0 + 2 * L]
            sw = sc * jnp.exp(jnp.where(masks[nn % 2], arg, -jnp.inf))
            wi = w_inter[nn][:, h:h + 1]
            num = jnp.dot(sw.astype(BF16), vb[p0:p0 + 2 * L], preferred_element_type=F32)
            num = num + wi * jnp.dot(qb_n, c_st.astype(BF16), preferred_element_type=F32)
            den = jnp.sum(sw, axis=1, keepdims=True) + wi * jnp.sum(q_n * n_st, axis=1, keepdims=True)
            den = jnp.maximum(jnp.abs(den), e_negm[nn][:, h:h + 1])
            h_ref[r0:r0 + L, h * M_HEAD_V:(h + 1) * M_HEAD_V] = num / den
            kw = k[r0:r0 + L] * w_k[nn][:, h:h + 1]
            wc = w_c[nn][:, h:h + 1]
            upd = lax.dot_general(kw.astype(BF16), vb[r0:r0 + L], (((0,), (0,)), ((), ())),
                                  preferred_element_type=F32)
            c_st = wc * c_st + upd
            n_st = wc * n_st + jnp.sum(kw, axis=0, keepdims=True)
        c_ref[h] = c_st
        n_ref[h] = n_st

    for h in range(M_HEADS):
        sl = slice(h * M_HEAD_V, (h + 1) * M_HEAD_V)
        hh = h_ref[:, sl]
        hn = hh * lax.rsqrt(jnp.mean(hh * hh, axis=-1, keepdims=True) + EPS) * hg_ref[:, sl]
        out = jax.nn.sigmoid(o_ref[:, sl]) * (hn + sk_ref[:, sl] * c[:, sl])
        out_ref[:, sl] = out.astype(out_ref.dtype)


def _mlstm(proj, cw, cb, wq, wk, ib, fb, hg, sk, bsz, t, tb=256):
    nt = t // tb
    gate_blk = (3 * M_WIDTH + proj.shape[1] - 3 * M_WIDTH - 2 * LANES) // LANES
    row = lambda b, i: b * nt + i
    full = lambda shape: pl.BlockSpec(shape, lambda b, i: (0,) * len(shape))
    return pl.pallas_call(
        functools.partial(_mlstm_body, tb=tb),
        grid=(bsz, nt),
        in_specs=[
            pl.BlockSpec((tb, M_WIDTH), lambda b, i: (row(b, i), 0)),
            pl.BlockSpec((tb, M_WIDTH), lambda b, i: (row(b, i), 1)),
            pl.BlockSpec((tb, M_WIDTH), lambda b, i: (row(b, i), 2)),
            pl.BlockSpec((tb, LANES), lambda b, i: (row(b, i), gate_blk)),
            pl.BlockSpec((tb, LANES), lambda b, i: (row(b, i), gate_blk + 1)),
            full((M_CONV, M_WIDTH)), full((1, M_WIDTH)),
            full((M_HEADS, M_HEAD_V, M_HEAD_QK)), full((M_HEADS, M_HEAD_V, M_HEAD_QK)),
            full((1, LANES)), full((1, LANES)), full((1, M_WIDTH)), full((1, M_WIDTH)),
        ],
        out_specs=pl.BlockSpec((tb, M_WIDTH), lambda b, i: (row(b, i), 0)),
        out_shape=jax.ShapeDtypeStruct((bsz * t, M_WIDTH), BF16),
        scratch_shapes=[
            pltpu.VMEM((tb + SUBLANES, M_WIDTH), F32),
            pltpu.VMEM((M_HEADS, M_HEAD_QK, M_HEAD_V), F32),
            pltpu.VMEM((M_HEADS, 1, M_HEAD_QK), F32),
            pltpu.VMEM((SUBLANES, LANES), F32),
            pltpu.VMEM((tb, M_WIDTH), F32),
        ],
        compiler_params=_cparams("parallel", "arbitrary"),
        name="mlstm",
    )(proj, proj, proj, proj, proj, cw, cb, wq, wk, ib, fb, hg, sk)


def _s5_body(u_ref, pre_ref, pim_ref, bre_ref, bim_ref, cre_ref, cim_ref, ar_ref, ai_ref, br_ref, bi_ref,
             tr_ref, ti_ref, y_ref, wst_ref, wct_ref, bbr_ref, toe_ref, ucat_ref, v_ref, s_ref, yall_ref,
             *, nl, nsg, tseg):
    lc = S5_CHUNK
    ns = S5_SLAB_STATE
    gch = S5_GROUP_CH
    nseq = SUBLANES

    @pl.when(pl.program_id(0) == 0)
    def _():
        wst_ref[...] = jnp.zeros_like(wst_ref)
        wct_ref[...] = jnp.zeros_like(wct_ref)
        bbr_ref[...] = jnp.zeros_like(bbr_ref)
        toe_ref[...] = jnp.zeros_like(toe_ref)

    for gg in range(S5_SLAB_GROUPS):
        k0 = (gg // 2) * LANES
        bre, bim, cre, cim = bre_ref[gg], bim_ref[gg], cre_ref[gg], cim_ref[gg]
        r0 = gg * gch
        bbr_ref[r0:r0 + gch, k0:k0 + LANES] = bre.astype(BF16)
        bbr_ref[r0:r0 + gch, ns + k0:ns + k0 + LANES] = bim.astype(BF16)
        for j in range(lc + 1):
            p_re = pre_ref[j, gg:gg + 1, :]
            p_im = pim_ref[j, gg:gg + 1, :]
            r1 = j * LANES + r0
            wct_ref[r1:r1 + gch, k0:k0 + LANES] = (cre * p_re - cim * p_im).astype(BF16)
            wct_ref[r1:r1 + gch, ns + k0:ns + k0 + LANES] = (-(cre * p_im + cim * p_re)).astype(BF16)
            if j < lc:
                r2 = (lc - 1 - j) * LANES + r0
                wst_ref[r2:r2 + gch, k0:k0 + LANES] = (bre * p_re - bim * p_im).astype(BF16)
                wst_ref[r2:r2 + gch, ns + k0:ns + k0 + LANES] = (bre * p_im + bim * p_re).astype(BF16)
    kt = lax.dot_general(bbr_ref[...], wct_ref[0:lc * LANES, :], (((1,), (1,)), ((), ())),
                         preferred_element_type=F32).astype(BF16)
    for i in range(lc):
        toe_ref[i * LANES:(i + 1) * LANES, i * LANES:] = kt[:, :(lc - i) * LANES]

    def gather(n, _):
        r0 = pl.multiple_of(n * nseq, nseq)
        for i in range(lc):
            ucat_ref[pl.ds(r0, nseq), i * LANES:(i + 1) * LANES] = u_ref[pl.ds(n * lc + i, nseq, stride=tseg), :]
        return 0

    lax.fori_loop(0, nl, gather, 0)
    uc = ucat_ref[...].astype(BF16)
    yall_ref[...] = jnp.dot(uc, toe_ref[...], preferred_element_type=F32)
    v_ref[...] = jnp.dot(uc, wst_ref[...], preferred_element_type=F32)

    ar = jnp.broadcast_to(ar_ref[...], (nseq, ns))
    ai = jnp.broadcast_to(ai_ref[...], (nseq, ns))

    def scan_step(n, carry):
        s_re, s_im = carry
        r0 = pl.multiple_of(n * nseq, nseq)
        s_ref[pl.ds(r0, nseq), 0:ns] = s_re
        s_ref[pl.ds(r0, nseq), ns:] = s_im
        n_re = ar * s_re - ai * s_im + v_ref[pl.ds(r0, nseq), 0:ns]
        n_im = ar * s_im + ai * s_re + v_ref[pl.ds(r0, nseq), ns:]
        return n_re, n_im

    zero = jnp.zeros((nseq, ns), F32)
    f_re, f_im = lax.fori_loop(0, nl, scan_step, (zero, zero))

    br = jnp.broadcast_to(br_ref[...], (nseq, ns))
    bi = jnp.broadcast_to(bi_ref[...], (nseq, ns))
    has_prev = lax.broadcasted_iota(jnp.int32, (nseq, ns), 0) % nsg >= 1
    i_re, i_im = zero, zero
    for _ in range(nsg - 1):
        t_re = f_re + br * i_re - bi * i_im
        t_im = f_im + br * i_im + bi * i_re
        i_re = jnp.where(has_prev, pltpu.roll(t_re, 1, axis=0), 0.0)
        i_im = jnp.where(has_prev, pltpu.roll(t_im, 1, axis=0), 0.0)

    def fix_step(n, _):
        r0 = pl.multiple_of(n * nseq, nseq)
        p_re = tr_ref[pl.ds(n, 1), :]
        p_im = ti_ref[pl.ds(n, 1), :]
        s_ref[pl.ds(r0, nseq), 0:ns] += p_re * i_re - p_im * i_im
        s_ref[pl.ds(r0, nseq), ns:] += p_re * i_im + p_im * i_re
        return 0

    lax.fori_loop(0, nl, fix_step, 0)

    yall_ref[...] += lax.dot_general(s_ref[...].astype(BF16), wct_ref[LANES:(lc + 1) * LANES, :],
                                     (((1,), (1,)), ((), ())), preferred_element_type=F32)

    def scatter(n, _):
        r0 = pl.multiple_of(n * nseq, nseq)
        for j in range(lc):
            y_ref[pl.ds(n * lc + j, nseq, stride=tseg), :] = yall_ref[pl.ds(r0, nseq), j * LANES:(j + 1) * LANES]
        return 0

    lax.fori_loop(0, nl, scatter, 0)


def _s5(proj, prm, bsz, t):
    bt = bsz * t
    nsg = SUBLANES // bsz
    tseg = t // nsg
    nl = tseg // S5_CHUNK
    r = nl * SUBLANES
    lc = S5_CHUNK
    ns = S5_SLAB_STATE
    n_slab = prm["bb_re"].shape[0] // S5_SLAB_GROUPS
    u_blk0 = 3 * M_WIDTH // LANES
    grp = lambda shape: pl.BlockSpec((S5_SLAB_GROUPS,) + shape, lambda s: (s,) + (0,) * len(shape))
    vec = lambda rows: pl.BlockSpec((rows, ns), lambda s: (0, s))
    return pl.pallas_call(
        functools.partial(_s5_body, nl=nl, nsg=nsg, tseg=tseg),
        grid=(n_slab,),
        in_specs=[
            pl.BlockSpec((bt, LANES), lambda s: (0, u_blk0 + s)),
            pl.BlockSpec((lc + 1, S5_SLAB_GROUPS, LANES), lambda s: (0, s, 0)),
            pl.BlockSpec((lc + 1, S5_SLAB_GROUPS, LANES), lambda s: (0, s, 0)),
            grp((S5_GROUP_CH, LANES)), grp((S5_GROUP_CH, LANES)), grp((S5_GROUP_CH, LANES)), grp((S5_GROUP_CH, LANES)),
            vec(1), vec(1), vec(1), vec(1), vec(nl), vec(nl),
        ],
        out_specs=pl.BlockSpec((bt, LANES), lambda s: (0, s)),
        out_shape=jax.ShapeDtypeStruct((bt, n_slab * LANES), F32),
        scratch_shapes=[
            pltpu.VMEM((lc * LANES, 2 * ns), BF16),
            pltpu.VMEM(((lc + 1) * LANES, 2 * ns), BF16),
            pltpu.VMEM((LANES, 2 * ns), BF16),
            pltpu.VMEM((lc * LANES, lc * LANES), BF16),
            pltpu.VMEM((r, lc * LANES), F32),
            pltpu.VMEM((r, 2 * ns), F32),
            pltpu.VMEM((r, 2 * ns), F32),
            pltpu.VMEM((r, lc * LANES), F32),
        ],
        compiler_params=_cparams("arbitrary"),
        name="s5",
    )(proj, prm["pw_re"], prm["pw_im"], prm["bb_re"], prm["bb_im"], prm["c_re"], prm["c_im"],
      prm["ac_re"], prm["ac_im"], prm["as_re"], prm["as_im"], prm["pt_re"], prm["pt_im"])


def _s5_params(a_re, a_im, log_dt, b_re, b_im, c_re, c_im, nl):
    lc = S5_CHUNK
    nlay, ng, npst = a_re.shape
    dt = jnp.exp(log_dt)[..., None]
    lam_re, lam_im = a_re * dt, a_im * dt

    def apow(n):
        n = n.astype(F32)[None, :, None, None]
        mag = jnp.exp(lam_re[:, None] * n)
        return mag * jnp.cos(lam_im[:, None] * n), mag * jnp.sin(lam_im[:, None] * n)

    ab_re, ab_im = jnp.exp(lam_re) * jnp.cos(lam_im), jnp.exp(lam_re) * jnp.sin(lam_im)
    den = a_re * a_re + a_im * a_im
    f_re = ((ab_re - 1.0) * a_re + ab_im * a_im) / den
    f_im = (ab_im * a_re - (ab_re - 1.0) * a_im) / den
    bb_re = (f_re[..., None] * b_re - f_im[..., None] * b_im).transpose(0, 1, 3, 2)
    bb_im = (f_re[..., None] * b_im + f_im[..., None] * b_re).transpose(0, 1, 3, 2)
    odd = (jnp.arange(ng) % 2 == 1)[None, :, None, None]

    def half(x):
        z = jnp.zeros_like(x)
        return jnp.concatenate([jnp.where(odd, z, x), jnp.where(odd, x, z)], axis=-1)

    both = lambda x: jnp.concatenate([x, x], axis=-1)
    flat = lambda x: x.reshape(nlay, x.shape[1], ng * npst)
    pw_re, pw_im = apow(jnp.arange(lc + 1))
    ac_re, ac_im = apow(jnp.array([lc]))
    as_re, as_im = apow(jnp.array([lc * nl]))
    pt_re, pt_im = apow(jnp.arange(nl) * lc)
    return {
        "pw_re": both(pw_re), "pw_im": both(pw_im),
        "bb_re": half(bb_re), "bb_im": half(bb_im), "c_re": half(c_re), "c_im": half(c_im),
        "ac_re": flat(ac_re), "ac_im": flat(ac_im), "as_re": flat(as_re), "as_im": flat(as_im),
        "pt_re": flat(pt_re), "pt_im": flat(pt_im),
    }


def _glu_body(y_ref, u_ref, d_ref, w_ref, b_ref, o_ref):
    y = jax.nn.gelu(y_ref[...] + d_ref[...] * u_ref[...])
    z = jnp.dot(y.astype(BF16), w_ref[...], preferred_element_type=F32) + b_ref[...]
    o_ref[...] = (y * jax.nn.sigmoid(z)).astype(o_ref.dtype)


def _glu(y, proj, d, w, b, tm=512):
    m, n = y.shape
    u_blk = 3 * M_WIDTH // n
    return pl.pallas_call(
        _glu_body,
        grid=(m // tm,),
        in_specs=[
            pl.BlockSpec((tm, n), lambda i: (i, 0)),
            pl.BlockSpec((tm, n), lambda i: (i, u_blk)),
            pl.BlockSpec((1, n), lambda i: (0, 0)),
            pl.BlockSpec((n, n), lambda i: (0, 0)),
            pl.BlockSpec((1, n), lambda i: (0, 0)),
        ],
        out_specs=pl.BlockSpec((tm, n), lambda i: (i, 0)),
        out_shape=jax.ShapeDtypeStruct((m, n), BF16),
        compiler_params=_cparams("parallel"),
        name="s5_glu",
    )(y, proj, d, w, b)


def _outproj_body(m_ref, s_ref, wm_ref, ws_ref, x_ref, g_ref, xo_ref, ho_ref):
    acc = jnp.dot(m_ref[...], wm_ref[...], preferred_element_type=F32)
    acc = acc + jnp.dot(s_ref[...], ws_ref[...], preferred_element_type=F32)
    x = x_ref[...] + acc
    xo_ref[...] = x
    ms = jnp.mean(x * x, axis=-1, keepdims=True)
    ho_ref[...] = (x * lax.rsqrt(ms + EPS) * g_ref[...]).astype(ho_ref.dtype)


def _outproj(m_out, s_out, w, x, g, tm=256):
    m, d = x.shape
    kh = m_out.shape[1]
    return pl.pallas_call(
        _outproj_body,
        grid=(m // tm,),
        in_specs=[
            pl.BlockSpec((tm, kh), lambda i: (i, 0)),
            pl.BlockSpec((tm, kh), lambda i: (i, 0)),
            pl.BlockSpec((kh, d), lambda i: (0, 0)),
            pl.BlockSpec((kh, d), lambda i: (1, 0)),
            pl.BlockSpec((tm, d), lambda i: (i, 0)),
            pl.BlockSpec((1, d), lambda i: (0, 0)),
        ],
        out_specs=[pl.BlockSpec((tm, d), lambda i: (i, 0)), pl.BlockSpec((tm, d), lambda i: (i, 0))],
        out_shape=[jax.ShapeDtypeStruct((m, d), F32), jax.ShapeDtypeStruct((m, d), BF16)],
        compiler_params=_cparams("parallel"),
        name="out_proj",
    )(m_out, s_out, w, w, x, g)


def _ffn_up_body(h_ref, wg_ref, wv_ref, cw_ref, cb_ref, a_ref, ext_ref, *, tm, tiles_per_seq):
    g = jnp.dot(h_ref[...], wg_ref[...], preferred_element_type=F32)

    @pl.when(pl.program_id(1) % tiles_per_seq == 0)
    def _():
        ext_ref[0:SUBLANES, :] = jnp.zeros((SUBLANES, g.shape[1]), F32)

    ext_ref[SUBLANES:, :] = g
    conv = cb_ref[...] + cw_ref[F_CONV - 1:F_CONV, :] * g
    for kk in range(F_CONV - 1):
        conv = conv + cw_ref[kk:kk + 1, :] * ext_ref[pl.ds(SUBLANES - (F_CONV - 1) + kk, tm), :]
    ext_ref[0:SUBLANES, :] = g[tm - SUBLANES:tm, :]
    val = jnp.dot(h_ref[...], wv_ref[...], preferred_element_type=F32)
    a_ref[...] = (jax.nn.gelu(conv) * val).astype(a_ref.dtype)


def _ffn_up(h, wg, wv, cw, cb, t, tm=1024, tn=512):
    m, k = h.shape
    n = wg.shape[1]
    return pl.pallas_call(
        functools.partial(_ffn_up_body, tm=tm, tiles_per_seq=t // tm),
        grid=(n // tn, m // tm),
        in_specs=[
            pl.BlockSpec((tm, k), lambda j, i: (i, 0)),
            pl.BlockSpec((k, tn), lambda j, i: (0, j)),
            pl.BlockSpec((k, tn), lambda j, i: (0, j)),
            pl.BlockSpec((F_CONV, tn), lambda j, i: (0, j)),
            pl.BlockSpec((1, tn), lambda j, i: (0, j)),
        ],
        out_specs=pl.BlockSpec((tm, tn), lambda j, i: (i, j)),
        out_shape=jax.ShapeDtypeStruct((m, n), BF16),
        scratch_shapes=[pltpu.VMEM((tm + SUBLANES, tn), F32)],
        compiler_params=_cparams("parallel", "arbitrary"),
        name="ffn_up",
    )(h, wg, wv, cw, cb)


def _ffn_down_body(a_ref, w_ref, x_ref, g_ref, xo_ref, ho_ref, acc_ref):
    kk = pl.program_id(1)

    @pl.when(kk == 0)
    def _():
        acc_ref[...] = x_ref[...]

    acc_ref[...] += jnp.dot(a_ref[...], w_ref[...], preferred_element_type=F32)

    @pl.when(kk == pl.num_programs(1) - 1)
    def _():
        x = acc_ref[...]
        xo_ref[...] = x
        ms = jnp.mean(x * x, axis=-1, keepdims=True)
        ho_ref[...] = (x * lax.rsqrt(ms + EPS) * g_ref[...]).astype(ho_ref.dtype)


def _ffn_down(a, w, x, g, h_dtype, tm=512, tk=512):
    m, d = x.shape
    k = a.shape[1]
    return pl.pallas_call(
        _ffn_down_body,
        grid=(m // tm, k // tk),
        in_specs=[
            pl.BlockSpec((tm, tk), lambda i, j: (i, j)),
            pl.BlockSpec((tk, d), lambda i, j: (j, 0)),
            pl.BlockSpec((tm, d), lambda i, j: (i, 0)),
            pl.BlockSpec((1, d), lambda i, j: (0, 0)),
        ],
        out_specs=[pl.BlockSpec((tm, d), lambda i, j: (i, 0)), pl.BlockSpec((tm, d), lambda i, j: (i, 0))],
        out_shape=[jax.ShapeDtypeStruct((m, d), F32), jax.ShapeDtypeStruct((m, d), h_dtype)],
        scratch_shapes=[pltpu.VMEM((tm, d), F32)],
        compiler_params=_cparams("parallel", "arbitrary"),
        name="ffn_down",
    )(a, w, x, g)


def _pad_cols(w, mult):
    pad = (-w.shape[-1]) % mult
    return jnp.pad(w, [(0, 0)] * (w.ndim - 1) + [(0, pad)])


def _trunk(x, norm_mix_g, w_in, m_conv_w, m_conv_b, w_q, w_k, m_i_bias, m_f_bias, m_head_g, m_skip,
           s5_a_re, s5_a_im, s5_log_dt, s5_b_re, s5_b_im, s5_c_re, s5_c_im, s5_d, s5_w_glu, s5_b_glu,
           w_out, norm_ffn_g, w_gate, w_val, f_conv_w, f_conv_b, w_down, norm_final_g):
    bsz, t, d = x.shape
    depth = w_in.shape[0]
    bt = bsz * t
    nl = t // (SUBLANES // bsz) // S5_CHUNK
    d_ff = w_gate.shape[2]

    c3 = 3 * M_WIDTH
    w_in_p = jnp.concatenate([
        w_in[:, :, :c3], w_in[:, :, c3 + 2 * M_HEADS:],
        _pad_cols(w_in[:, :, c3:c3 + M_HEADS], LANES), _pad_cols(w_in[:, :, c3 + M_HEADS:c3 + 2 * M_HEADS], LANES),
    ], axis=-1).astype(BF16)
    ib = _pad_cols(m_i_bias, LANES)[:, None, :]
    fb = _pad_cols(m_f_bias, LANES)[:, None, :]
    wq_b, wk_b = w_q.astype(BF16), w_k.astype(BF16)
    s5p = _s5_params(s5_a_re, s5_a_im, s5_log_dt, s5_b_re, s5_b_im, s5_c_re, s5_c_im, nl)
    wglu_b = s5_w_glu.astype(BF16)
    wout_b = w_out.astype(BF16)
    ff_mult = 512
    wg_b = _pad_cols(w_gate, ff_mult).astype(BF16)
    wv_b = _pad_cols(w_val, ff_mult).astype(BF16)
    fcw = _pad_cols(f_conv_w, ff_mult)
    fcb = _pad_cols(f_conv_b, ff_mult)[:, None, :]
    d_ffp = wg_b.shape[2]
    wd_b = jnp.pad(w_down, ((0, 0), (0, d_ffp - d_ff), (0, 0))).astype(BF16)

    xf = x.reshape(bt, d)
    h = _rmsnorm(xf, norm_mix_g[0], BF16)
    out = None
    for l in range(depth):
        proj = _matmul(h, w_in_p[l], F32, tm=1024, tn=256)
        m_out = _mlstm(proj, m_conv_w[l], m_conv_b[l][None], wq_b[l], wk_b[l], ib[l], fb[l],
                       m_head_g[l][None], m_skip[l][None], bsz, t)
        y = _s5(proj, {k_: v_[l] for k_, v_ in s5p.items()}, bsz, t)
        s_out = _glu(y, proj, s5_d[l][None], wglu_b[l], s5_b_glu[l][None])
        xf, h = _outproj(m_out, s_out, wout_b[l], xf, norm_ffn_g[l][None])
        a = _ffn_up(h, wg_b[l], wv_b[l], fcw[l], fcb[l], t)
        last = l == depth - 1
        g_next = norm_final_g if last else norm_mix_g[l + 1]
        xf, h = _ffn_down(a, wd_b[l], xf, g_next[None], F32 if last else BF16)
        out = h
    return out.reshape(bsz, t, d)


def kernel(x, norm_mix_g, w_in, m_conv_w, m_conv_b, w_q, w_k, m_i_bias, m_f_bias, m_head_g, m_skip, s5_a_re, s5_a_im, s5_log_dt, s5_b_re, s5_b_im, s5_c_re, s5_c_im, s5_d, s5_w_glu, s5_b_glu, w_out, norm_ffn_g, w_gate, w_val, f_conv_w, f_conv_b, w_down, norm_final_g):
    return _trunk(x, norm_mix_g, w_in, m_conv_w, m_conv_b, w_q, w_k, m_i_bias, m_f_bias, m_head_g, m_skip,
                  s5_a_re, s5_a_im, s5_log_dt, s5_b_re, s5_b_im, s5_c_re, s5_c_im, s5_d, s5_w_glu, s5_b_glu,
                  w_out, norm_ffn_g, w_gate, w_val, f_conv_w, f_conv_b, w_down, norm_final_g)
```

```python
import functools
import math

import jax
import jax.numpy as jnp
from jax import lax
from jax.experimental import pallas as pl
from jax.experimental.pallas import tpu as pltpu

F32 = jnp.float32
BF16 = jnp.bfloat16
EPS = 1e-6

M_HEADS = 4
M_HEAD_V = 256
M_HEAD_QK = 128
M_WIDTH = M_HEADS * M_HEAD_V
M_CONV = 4
M_CHUNK = 64
S5_GROUP_CH = 16
S5_STATE = 64
S5_CH = 1024
S5_CHUNK = 8
F_CONV = 3
PROJ_U_COL = 3 * M_WIDTH
PROJ_GATE_COL = PROJ_U_COL + S5_CH
PROJ_COLS = 4608

LANES = 128
SUBLANES = 8
S5_SLAB_GROUPS = LANES // S5_GROUP_CH
S5_SLAB_STATE = S5_SLAB_GROUPS * S5_STATE
VMEM_LIMIT = 56 * 1024 * 1024


def _cparams(*sem):
    return pltpu.CompilerParams(dimension_semantics=sem, vmem_limit_bytes=VMEM_LIMIT)


def _rmsnorm_body(x_ref, g_ref, o_ref):
    x = x_ref[...]
    ms = jnp.mean(x * x, axis=-1, keepdims=True)
    o_ref[...] = (x * lax.rsqrt(ms + EPS) * g_ref[...]).astype(o_ref.dtype)


def _rmsnorm(x, g, out_dtype, tm=512):
    m, d = x.shape
    return pl.pallas_call(
        _rmsnorm_body,
        grid=(m // tm,),
        in_specs=[pl.BlockSpec((tm, d), lambda i: (i, 0)), pl.BlockSpec((1, d), lambda i: (0, 0))],
        out_specs=pl.BlockSpec((tm, d), lambda i: (i, 0)),
        out_shape=jax.ShapeDtypeStruct((m, d), out_dtype),
        compiler_params=_cparams("parallel"),
        name="rmsnorm",
    )(x, g.reshape(1, d))


def _in_proj_body(x_ref, g_ref, w_ref, o_ref, h_ref):
    @pl.when(pl.program_id(1) == 0)
    def _():
        x = x_ref[...]
        ms = jnp.mean(x * x, axis=-1, keepdims=True)
        h_ref[...] = (x * lax.rsqrt(ms + EPS) * g_ref[...]).astype(BF16)

    o_ref[...] = jnp.dot(h_ref[...], w_ref[...], preferred_element_type=F32)


def _in_proj(x, g, w, l, tm=1024, tn=768):
    m, k = x.shape
    n = w.shape[2]
    return pl.pallas_call(
        _in_proj_body,
        grid=(m // tm, n // tn),
        in_specs=[
            pl.BlockSpec((tm, k), lambda i, j: (i, 0)),
            pl.BlockSpec((1, k), lambda i, j: (0, 0)),
            pl.BlockSpec((None, k, tn), lambda i, j: (l, 0, j)),
        ],
        out_specs=pl.BlockSpec((tm, tn), lambda i, j: (i, j)),
        out_shape=jax.ShapeDtypeStruct((m, n), F32),
        scratch_shapes=[pltpu.VMEM((tm, k), BF16)],
        compiler_params=_cparams("parallel", "arbitrary"),
        name="in_proj",
    )(x, g, w)


def _log_sigmoid(x):
    return jnp.minimum(x, 0.0) - jnp.log1p(jnp.exp(-jnp.abs(x)))


def _mlstm_body(xm_ref, v_ref, o_ref, ig_ref, fg_ref, cw_ref, cb_ref, wq_ref, wk_ref, ib_ref, fb_ref,
                hg_ref, sk_ref, out_ref, ext_ref, c_ref, n_ref, m_ref, h_ref, *, tb):
    ncb = tb // M_CHUNK
    L = M_CHUNK

    @pl.when(pl.program_id(1) == 0)
    def _():
        ext_ref[0:SUBLANES, :] = jnp.zeros((SUBLANES, M_WIDTH), F32)
        c_ref[...] = jnp.zeros_like(c_ref)
        n_ref[...] = jnp.zeros_like(n_ref)
        m_ref[...] = jnp.zeros_like(m_ref)

    ext_ref[SUBLANES:, :] = xm_ref[...]
    acc = cb_ref[...] + cw_ref[0:1, :] * ext_ref[pl.ds(SUBLANES - 3, tb), :]
    for kk in range(1, M_CONV):
        acc = acc + cw_ref[kk:kk + 1, :] * ext_ref[pl.ds(SUBLANES - 3 + kk, tb), :]
    c = acc * jax.nn.sigmoid(acc)
    ext_ref[0:SUBLANES, :] = xm_ref[tb - SUBLANES:tb, :]

    i_pre = ig_ref[...] + ib_ref[...]
    lf = _log_sigmoid(fg_ref[...] + fb_ref[...])
    pos = lax.broadcasted_iota(jnp.int32, (tb, LANES), 0) % L
    bc = lf
    s = 1
    while s < L:
        bc = bc + jnp.where(pos >= s, pltpu.roll(bc, s, axis=0), 0.0)
        s *= 2
    rowv = i_pre - bc
    a = rowv
    s = 1
    while s < L:
        a = jnp.where(pos >= s, jnp.maximum(a, pltpu.roll(a, s, axis=0)), a)
        s *= 2
    bc3 = bc.reshape(ncb, L, LANES)
    g = bc3[:, L - 1:L, :]
    dec3 = g - bc3 + i_pre.reshape(ncb, L, LANES)
    maxdec = jnp.max(dec3, axis=1, keepdims=True)
    m_run = m_ref[0:1, :]
    m0_l, m1_l = [], []
    for nn in range(ncb):
        m0_l.append(m_run)
        m_run = jnp.maximum(g[nn] + m_run, maxdec[nn])
        m1_l.append(m_run)
    m_ref[0:1, :] = m_run
    m0 = jnp.stack(m0_l, axis=0)
    m1 = jnp.stack(m1_l, axis=0)
    mx = jnp.maximum(m0, a.reshape(ncb, L, LANES))
    colv = -mx
    w_inter = jnp.exp(m0 - mx)
    e_negm = jnp.exp(-(bc3 + mx))
    w_k = jnp.exp(dec3 - m1)
    w_c = jnp.exp(g + m0 - m1)
    rowv_t = rowv.T

    ii = lax.broadcasted_iota(jnp.int32, (L, 2 * L), 0)
    jj = lax.broadcasted_iota(jnp.int32, (L, 2 * L), 1)
    masks = [(jj <= ii), (jj >= L) & (jj - L <= ii)]
    scale = M_HEAD_QK ** -0.5

    for h in range(M_HEADS):
        c_h = c[:, h * M_HEAD_V:(h + 1) * M_HEAD_V].astype(BF16)
        q = jnp.dot(c_h, wq_ref[h], preferred_element_type=F32) * scale
        k = jnp.dot(c_h, wk_ref[h], preferred_element_type=F32)
        qb = q.astype(BF16)
        kb = k.astype(BF16)
        vb = v_ref[:, h * M_HEAD_V:(h + 1) * M_HEAD_V].astype(BF16)
        c_st = c_ref[h]
        n_st = n_ref[h]
        for nn in range(ncb):
            p0 = (nn // 2) * 2 * L
            r0 = nn * L
            q_n = q[r0:r0 + L]
            qb_n = qb[r0:r0 + L]
            sc = lax.dot_general(qb_n, kb[p0:p0 + 2 * L], (((1,), (1,)), ((), ())),
                                 preferred_element_type=F32)
            arg = colv[nn][:, h:h + 1] + rowv_t[h:h + 1, p0:p0 + 2 * L]
            sw = sc * jnp.exp(jnp.where(masks[nn % 2], arg, -jnp.inf))
            wi = w_inter[nn][:, h:h + 1]
            num = jnp.dot(sw.astype(BF16), vb[p0:p0 + 2 * L], preferred_element_type=F32)
            num = num + wi * jnp.dot(qb_n, c_st.astype(BF16), preferred_element_type=F32)
            den = jnp.sum(sw, axis=1, keepdims=True) + wi * jnp.sum(q_n * n_st, axis=1, keepdims=True)
            den = jnp.maximum(jnp.abs(den), e_negm[nn][:, h:h + 1])
            h_ref[r0:r0 + L, h * M_HEAD_V:(h + 1) * M_HEAD_V] = num / den
            kw = k[r0:r0 + L] * w_k[nn][:, h:h + 1]
            wc = w_c[nn][:, h:h + 1]
            upd = lax.dot_general(kw.astype(BF16), vb[r0:r0 + L], (((0,), (0,)), ((), ())),
                                  preferred_element_type=F32)
            c_st = wc * c_st + upd
            n_st = wc * n_st + jnp.sum(kw, axis=0, keepdims=True)
        c_ref[h] = c_st
        n_ref[h] = n_st

    for h in range(M_HEADS):
        sl = slice(h * M_HEAD_V, (h + 1) * M_HEAD_V)
        hh = h_ref[:, sl]
        hn = hh * lax.rsqrt(jnp.mean(hh * hh, axis=-1, keepdims=True) + EPS) * hg_ref[:, sl]
        out = jax.nn.sigmoid(o_ref[:, sl]) * (hn + sk_ref[:, sl] * c[:, sl])
        out_ref[:, sl] = out.astype(out_ref.dtype)


def _mlstm(proj, cw, cb, wq, wk, ib, fb, hg, sk, bsz, t, tb=256):
    nt = t // tb
    gate_blk = PROJ_GATE_COL // LANES
    row = lambda b, i: b * nt + i
    full = lambda shape: pl.BlockSpec(shape, lambda b, i: (0,) * len(shape))
    return pl.pallas_call(
        functools.partial(_mlstm_body, tb=tb),
        grid=(bsz, nt),
        in_specs=[
            pl.BlockSpec((tb, M_WIDTH), lambda b, i: (row(b, i), 0)),
            pl.BlockSpec((tb, M_WIDTH), lambda b, i: (row(b, i), 1)),
            pl.BlockSpec((tb, M_WIDTH), lambda b, i: (row(b, i), 2)),
            pl.BlockSpec((tb, LANES), lambda b, i: (row(b, i), gate_blk)),
            pl.BlockSpec((tb, LANES), lambda b, i: (row(b, i), gate_blk + 1)),
            full((M_CONV, M_WIDTH)), full((1, M_WIDTH)),
            full((M_HEADS, M_HEAD_V, M_HEAD_QK)), full((M_HEADS, M_HEAD_V, M_HEAD_QK)),
            full((1, LANES)), full((1, LANES)), full((1, M_WIDTH)), full((1, M_WIDTH)),
        ],
        out_specs=pl.BlockSpec((tb, M_WIDTH), lambda b, i: (row(b, i), 0)),
        out_shape=jax.ShapeDtypeStruct((bsz * t, M_WIDTH), BF16),
        scratch_shapes=[
            pltpu.VMEM((tb + SUBLANES, M_WIDTH), F32),
            pltpu.VMEM((M_HEADS, M_HEAD_QK, M_HEAD_V), F32),
            pltpu.VMEM((M_HEADS, 1, M_HEAD_QK), F32),
            pltpu.VMEM((SUBLANES, LANES), F32),
            pltpu.VMEM((tb, M_WIDTH), F32),
        ],
        compiler_params=_cparams("parallel", "arbitrary"),
        name="mlstm",
    )(proj, proj, proj, proj, proj, cw, cb, wq, wk, ib, fb, hg, sk)


def _s5_body(u_ref, pre_ref, pim_ref, bre_ref, bim_ref, cre_ref, cim_ref, ar_ref, ai_ref, br_ref, bi_ref,
             tr_ref, ti_ref, y_ref, wst_ref, wct_ref, bbr_ref, toe_ref, ucat_ref, v_ref, s_ref, yall_ref,
             *, nl, nsg, tseg):
    lc = S5_CHUNK
    ns = S5_SLAB_STATE
    gch = S5_GROUP_CH
    nseq = SUBLANES

    @pl.when(pl.program_id(0) == 0)
    def _():
        wst_ref[...] = jnp.zeros_like(wst_ref)
        wct_ref[...] = jnp.zeros_like(wct_ref)
        bbr_ref[...] = jnp.zeros_like(bbr_ref)
        toe_ref[...] = jnp.zeros_like(toe_ref)

    for gg in range(S5_SLAB_GROUPS):
        k0 = (gg // 2) * LANES
        bre, bim, cre, cim = bre_ref[gg], bim_ref[gg], cre_ref[gg], cim_ref[gg]
        r0 = gg * gch
        bbr_ref[r0:r0 + gch, k0:k0 + LANES] = bre.astype(BF16)
        bbr_ref[r0:r0 + gch, ns + k0:ns + k0 + LANES] = bim.astype(BF16)
        for j in range(lc + 1):
            p_re = pre_ref[j, gg:gg + 1, :]
            p_im = pim_ref[j, gg:gg + 1, :]
            r1 = j * LANES + r0
            wct_ref[r1:r1 + gch, k0:k0 + LANES] = (cre * p_re - cim * p_im).astype(BF16)
            wct_ref[r1:r1 + gch, ns + k0:ns + k0 + LANES] = (-(cre * p_im + cim * p_re)).astype(BF16)
            if j < lc:
                r2 = (lc - 1 - j) * LANES + r0
                wst_ref[r2:r2 + gch, k0:k0 + LANES] = (bre * p_re - bim * p_im).astype(BF16)
                wst_ref[r2:r2 + gch, ns + k0:ns + k0 + LANES] = (bre * p_im + bim * p_re).astype(BF16)
    kt = lax.dot_general(bbr_ref[...], wct_ref[0:lc * LANES, :], (((1,), (1,)), ((), ())),
                         preferred_element_type=F32).astype(BF16)
    for i in range(lc):
        toe_ref[i * LANES:(i + 1) * LANES, i * LANES:] = kt[:, :(lc - i) * LANES]

    def gather(n, _):
        r0 = pl.multiple_of(n * nseq, nseq)
        for i in range(lc):
            ucat_ref[pl.ds(r0, nseq), i * LANES:(i + 1) * LANES] = u_ref[pl.ds(n * lc + i, nseq, stride=tseg), :]
        return 0

    lax.fori_loop(0, nl, gather, 0)
    uc = ucat_ref[...].astype(BF16)
    yall_ref[...] = jnp.dot(uc, toe_ref[...], preferred_element_type=F32)
    v_ref[...] = jnp.dot(uc, wst_ref[...], preferred_element_type=F32)

    ar = jnp.broadcast_to(ar_ref[...], (nseq, ns))
    ai = jnp.broadcast_to(ai_ref[...], (nseq, ns))

    def scan_step(n, carry):
        s_re, s_im = carry
        r0 = pl.multiple_of(n * nseq, nseq)
        s_ref[pl.ds(r0, nseq), 0:ns] = s_re
        s_ref[pl.ds(r0, nseq), ns:] = s_im
        n_re = ar * s_re - ai * s_im + v_ref[pl.ds(r0, nseq), 0:ns]
        n_im = ar * s_im + ai * s_re + v_ref[pl.ds(r0, nseq), ns:]
        return n_re, n_im

    zero = jnp.zeros((nseq, ns), F32)
    f_re, f_im = lax.fori_loop(0, nl, scan_step, (zero, zero))

    br = jnp.broadcast_to(br_ref[...], (nseq, ns))
    bi = jnp.broadcast_to(bi_ref[...], (nseq, ns))
    has_prev = lax.broadcasted_iota(jnp.int32, (nseq, ns), 0) % nsg >= 1
    i_re, i_im = zero, zero
    for _ in range(nsg - 1):
        t_re = f_re + br * i_re - bi * i_im
        t_im = f_im + br * i_im + bi * i_re
        i_re = jnp.where(has_prev, pltpu.roll(t_re, 1, axis=0), 0.0)
        i_im = jnp.where(has_prev, pltpu.roll(t_im, 1, axis=0), 0.0)

    def fix_step(n, _):
        r0 = pl.multiple_of(n * nseq, nseq)
        p_re = tr_ref[pl.ds(n, 1), :]
        p_im = ti_ref[pl.ds(n, 1), :]
        s_ref[pl.ds(r0, nseq), 0:ns] += p_re * i_re - p_im * i_im
        s_ref[pl.ds(r0, nseq), ns:] += p_re * i_im + p_im * i_re
        return 0

    lax.fori_loop(0, nl, fix_step, 0)

    yall_ref[...] += lax.dot_general(s_ref[...].astype(BF16), wct_ref[LANES:(lc + 1) * LANES, :],
                                     (((1,), (1,)), ((), ())), preferred_element_type=F32)

    def scatter(n, _):
        r0 = pl.multiple_of(n * nseq, nseq)
        for j in range(lc):
            y_ref[pl.ds(n * lc + j, nseq, stride=tseg), :] = yall_ref[pl.ds(r0, nseq), j * LANES:(j + 1) * LANES]
        return 0

    lax.fori_loop(0, nl, scatter, 0)


def _s5(proj, prm, bsz, t):
    bt = bsz * t
    nsg = SUBLANES // bsz
    tseg = t // nsg
    nl = tseg // S5_CHUNK
    r = nl * SUBLANES
    lc = S5_CHUNK
    ns = S5_SLAB_STATE
    n_slab = prm["bb_re"].shape[0] // S5_SLAB_GROUPS
    u_blk0 = PROJ_U_COL // LANES
    grp = lambda shape: pl.BlockSpec((S5_SLAB_GROUPS,) + shape, lambda s: (s,) + (0,) * len(shape))
    vec = lambda rows: pl.BlockSpec((rows, ns), lambda s: (0, s))
    return pl.pallas_call(
        functools.partial(_s5_body, nl=nl, nsg=nsg, tseg=tseg),
        grid=(n_slab,),
        in_specs=[
            pl.BlockSpec((bt, LANES), lambda s: (0, u_blk0 + s)),
            pl.BlockSpec((lc + 1, S5_SLAB_GROUPS, LANES), lambda s: (0, s, 0)),
            pl.BlockSpec((lc + 1, S5_SLAB_GROUPS, LANES), lambda s: (0, s, 0)),
            grp((S5_GROUP_CH, LANES)), grp((S5_GROUP_CH, LANES)), grp((S5_GROUP_CH, LANES)), grp((S5_GROUP_CH, LANES)),
            vec(1), vec(1), vec(1), vec(1), vec(nl), vec(nl),
        ],
        out_specs=pl.BlockSpec((bt, LANES), lambda s: (0, s)),
        out_shape=jax.ShapeDtypeStruct((bt, n_slab * LANES), F32),
        scratch_shapes=[
            pltpu.VMEM((lc * LANES, 2 * ns), BF16),
            pltpu.VMEM(((lc + 1) * LANES, 2 * ns), BF16),
            pltpu.VMEM((LANES, 2 * ns), BF16),
            pltpu.VMEM((lc * LANES, lc * LANES), BF16),
            pltpu.VMEM((r, lc * LANES), F32),
            pltpu.VMEM((r, 2 * ns), F32),
            pltpu.VMEM((r, 2 * ns), F32),
            pltpu.VMEM((r, lc * LANES), F32),
        ],
        compiler_params=_cparams("arbitrary"),
        name="s5",
    )(proj, prm["pw_re"], prm["pw_im"], prm["bb_re"], prm["bb_im"], prm["c_re"], prm["c_im"],
      prm["ac_re"], prm["ac_im"], prm["as_re"], prm["as_im"], prm["pt_re"], prm["pt_im"])


def _s5_params(a_re, a_im, log_dt, b_re, b_im, c_re, c_im, nl):
    lc = S5_CHUNK
    nlay, ng, npst = a_re.shape
    dt = jnp.exp(log_dt)[..., None]
    lam_re, lam_im = a_re * dt, a_im * dt

    def apow(n):
        n = n.astype(F32)[None, :, None, None]
        mag = jnp.exp(lam_re[:, None] * n)
        return mag * jnp.cos(lam_im[:, None] * n), mag * jnp.sin(lam_im[:, None] * n)

    ab_re, ab_im = jnp.exp(lam_re) * jnp.cos(lam_im), jnp.exp(lam_re) * jnp.sin(lam_im)
    den = a_re * a_re + a_im * a_im
    f_re = ((ab_re - 1.0) * a_re + ab_im * a_im) / den
    f_im = (ab_im * a_re - (ab_re - 1.0) * a_im) / den
    bb_re = (f_re[..., None] * b_re - f_im[..., None] * b_im).transpose(0, 1, 3, 2)
    bb_im = (f_re[..., None] * b_im + f_im[..., None] * b_re).transpose(0, 1, 3, 2)
    odd = (jnp.arange(ng) % 2 == 1)[None, :, None, None]

    def half(x):
        z = jnp.zeros_like(x)
        return jnp.concatenate([jnp.where(odd, z, x), jnp.where(odd, x, z)], axis=-1)

    both = lambda x: jnp.concatenate([x, x], axis=-1)
    flat = lambda x: x.reshape(nlay, x.shape[1], ng * npst)
    pw_re, pw_im = apow(jnp.arange(lc + 1))
    ac_re, ac_im = apow(jnp.array([lc]))
    as_re, as_im = apow(jnp.array([lc * nl]))
    pt_re, pt_im = apow(jnp.arange(nl) * lc)
    return {
        "pw_re": both(pw_re), "pw_im": both(pw_im),
        "bb_re": half(bb_re), "bb_im": half(bb_im), "c_re": half(c_re), "c_im": half(c_im),
        "ac_re": flat(ac_re), "ac_im": flat(ac_im), "as_re": flat(as_re), "as_im": flat(as_im),
        "pt_re": flat(pt_re), "pt_im": flat(pt_im),
    }


def _glu_body(y_ref, u_ref, d_ref, w_ref, b_ref, o_ref):
    y = jax.nn.gelu(y_ref[...] + d_ref[...] * u_ref[...])
    z = jnp.dot(y.astype(BF16), w_ref[...], preferred_element_type=F32) + b_ref[...]
    o_ref[...] = (y * jax.nn.sigmoid(z)).astype(o_ref.dtype)


def _glu(y, proj, d, w, l, b, tm=512):
    m, n = y.shape
    u_blk = PROJ_U_COL // n
    return pl.pallas_call(
        _glu_body,
        grid=(m // tm,),
        in_specs=[
            pl.BlockSpec((tm, n), lambda i: (i, 0)),
            pl.BlockSpec((tm, n), lambda i: (i, u_blk)),
            pl.BlockSpec((1, n), lambda i: (0, 0)),
            pl.BlockSpec((None, n, n), lambda i: (l, 0, 0)),
            pl.BlockSpec((1, n), lambda i: (0, 0)),
        ],
        out_specs=pl.BlockSpec((tm, n), lambda i: (i, 0)),
        out_shape=jax.ShapeDtypeStruct((m, n), BF16),
        compiler_params=_cparams("parallel"),
        name="s5_glu",
    )(y, proj, d, w, b)


def _outproj_body(m_ref, s_ref, wm_ref, ws_ref, x_ref, g_ref, xo_ref, ho_ref):
    acc = jnp.dot(m_ref[...], wm_ref[...], preferred_element_type=F32)
    acc = acc + jnp.dot(s_ref[...], ws_ref[...], preferred_element_type=F32)
    x = x_ref[...] + acc
    xo_ref[...] = x
    ms = jnp.mean(x * x, axis=-1, keepdims=True)
    ho_ref[...] = (x * lax.rsqrt(ms + EPS) * g_ref[...]).astype(ho_ref.dtype)


def _outproj(m_out, s_out, w, l, x, g, tm=512):
    m, d = x.shape
    kh = m_out.shape[1]
    return pl.pallas_call(
        _outproj_body,
        grid=(m // tm,),
        in_specs=[
            pl.BlockSpec((tm, kh), lambda i: (i, 0)),
            pl.BlockSpec((tm, kh), lambda i: (i, 0)),
            pl.BlockSpec((None, kh, d), lambda i: (l, 0, 0)),
            pl.BlockSpec((None, kh, d), lambda i: (l, 1, 0)),
            pl.BlockSpec((tm, d), lambda i: (i, 0)),
            pl.BlockSpec((1, d), lambda i: (0, 0)),
        ],
        out_specs=[pl.BlockSpec((tm, d), lambda i: (i, 0)), pl.BlockSpec((tm, d), lambda i: (i, 0))],
        out_shape=[jax.ShapeDtypeStruct((m, d), F32), jax.ShapeDtypeStruct((m, d), BF16)],
        compiler_params=_cparams("parallel"),
        name="out_proj",
    )(m_out, s_out, w, w, x, g)


def _ffn_up_body(h_ref, wg_ref, wv_ref, cw_ref, cb_ref, a_ref, wgb_ref, wvb_ref, ext_ref, *, tm, ts, tiles_per_seq):
    i = pl.program_id(1)

    @pl.when(i == 0)
    def _():
        wgb_ref[...] = wg_ref[...].astype(BF16)
        wvb_ref[...] = wv_ref[...].astype(BF16)

    @pl.when(i % tiles_per_seq == 0)
    def _():
        ext_ref[0:SUBLANES, :] = jnp.zeros((SUBLANES, ext_ref.shape[1]), F32)

    for r0 in range(0, tm, ts):
        hs = h_ref[r0:r0 + ts, :]
        g = jnp.dot(hs, wgb_ref[...], preferred_element_type=F32)
        ext_ref[SUBLANES + r0:SUBLANES + r0 + ts, :] = g
        conv = cb_ref[...] + cw_ref[F_CONV - 1:F_CONV, :] * g
        for kk in range(F_CONV - 1):
            conv = conv + cw_ref[kk:kk + 1, :] * ext_ref[pl.ds(SUBLANES + r0 - (F_CONV - 1) + kk, ts), :]
        val = jnp.dot(hs, wvb_ref[...], preferred_element_type=F32)
        a_ref[r0:r0 + ts, :] = (jax.nn.gelu(conv) * val).astype(a_ref.dtype)
    ext_ref[0:SUBLANES, :] = ext_ref[tm:tm + SUBLANES, :]


def _ffn_up(h, wg, wv, cw, cb, l, t, tm=1024, tn=512, ts=256):
    m, k = h.shape
    n = wg.shape[2]
    return pl.pallas_call(
        functools.partial(_ffn_up_body, tm=tm, ts=ts, tiles_per_seq=t // tm),
        grid=(pl.cdiv(n, tn), m // tm),
        in_specs=[
            pl.BlockSpec((tm, k), lambda j, i: (i, 0)),
            pl.BlockSpec((None, k, tn), lambda j, i: (l, 0, j)),
            pl.BlockSpec((None, k, tn), lambda j, i: (l, 0, j)),
            pl.BlockSpec((None, F_CONV, tn), lambda j, i: (l, 0, j)),
            pl.BlockSpec((None, 1, tn), lambda j, i: (l, 0, j)),
        ],
        out_specs=pl.BlockSpec((tm, tn), lambda j, i: (i, j)),
        out_shape=jax.ShapeDtypeStruct((m, n), BF16),
        scratch_shapes=[
            pltpu.VMEM((k, tn), BF16), pltpu.VMEM((k, tn), BF16),
            pltpu.VMEM((tm + SUBLANES, tn), F32),
        ],
        compiler_params=_cparams("parallel", "arbitrary"),
        name="ffn_up",
    )(h, wg, wv, cw, cb)


def _ffn_down_body(a_ref, w_ref, x_ref, xo_ref, wb_ref):
    @pl.when(pl.program_id(1) == 0)
    def _():
        wb_ref[...] = w_ref[...].astype(BF16)

    xo_ref[...] = x_ref[...] + jnp.dot(a_ref[...], wb_ref[...], preferred_element_type=F32)


def _ffn_down(a, w, l, x, tm=512, tn=512):
    m, d = x.shape
    k = a.shape[1]
    return pl.pallas_call(
        _ffn_down_body,
        grid=(d // tn, m // tm),
        in_specs=[
            pl.BlockSpec((tm, k), lambda j, i: (i, 0)),
            pl.BlockSpec((None, k, tn), lambda j, i: (l, 0, j)),
            pl.BlockSpec((tm, tn), lambda j, i: (i, j)),
        ],
        out_specs=pl.BlockSpec((tm, tn), lambda j, i: (i, j)),
        out_shape=jax.ShapeDtypeStruct((m, d), F32),
        scratch_shapes=[pltpu.VMEM((k, tn), BF16)],
        compiler_params=_cparams("parallel", "arbitrary"),
        name="ffn_down",
    )(a, w, x)


def _pad_cols(w, mult):
    pad = (-w.shape[-1]) % mult
    return jnp.pad(w, [(0, 0)] * (w.ndim - 1) + [(0, pad)])


def _trunk(x, norm_mix_g, w_in, m_conv_w, m_conv_b, w_q, w_k, m_i_bias, m_f_bias, m_head_g, m_skip,
           s5_a_re, s5_a_im, s5_log_dt, s5_b_re, s5_b_im, s5_c_re, s5_c_im, s5_d, s5_w_glu, s5_b_glu,
           w_out, norm_ffn_g, w_gate, w_val, f_conv_w, f_conv_b, w_down, norm_final_g):
    bsz, t, d = x.shape
    depth = w_in.shape[0]
    bt = bsz * t
    nl = t // (SUBLANES // bsz) // S5_CHUNK

    c3 = PROJ_U_COL
    w_in_p = jnp.concatenate([
        w_in[:, :, :c3], w_in[:, :, c3 + 2 * M_HEADS:],
        _pad_cols(w_in[:, :, c3:c3 + M_HEADS], LANES), _pad_cols(w_in[:, :, c3 + M_HEADS:c3 + 2 * M_HEADS], LANES),
    ], axis=-1)
    w_in_p = _pad_cols(w_in_p, PROJ_COLS).astype(BF16)
    ib = _pad_cols(m_i_bias, LANES)[:, None, :]
    fb = _pad_cols(m_f_bias, LANES)[:, None, :]
    wq_b, wk_b = w_q.astype(BF16), w_k.astype(BF16)
    s5p = _s5_params(s5_a_re, s5_a_im, s5_log_dt, s5_b_re, s5_b_im, s5_c_re, s5_c_im, nl)
    wglu_b = s5_w_glu.astype(BF16)
    wout_b = w_out.astype(BF16)

    xf = x.reshape(bt, d)
    for l in range(depth):
        proj = _in_proj(xf, norm_mix_g[l][None], w_in_p, l)
        m_out = _mlstm(proj, m_conv_w[l], m_conv_b[l][None], wq_b[l], wk_b[l], ib[l], fb[l],
                       m_head_g[l][None], m_skip[l][None], bsz, t)
        y = _s5(proj, {k_: v_[l] for k_, v_ in s5p.items()}, bsz, t)
        s_out = _glu(y, proj, s5_d[l][None], wglu_b, l, s5_b_glu[l][None])
        xf, h = _outproj(m_out, s_out, wout_b, l, xf, norm_ffn_g[l][None])
        a = _ffn_up(h, w_gate, w_val, f_conv_w, f_conv_b[:, None, :], l, t)
        xf = _ffn_down(a, w_down, l, xf)
    return _rmsnorm(xf, norm_final_g, F32).reshape(bsz, t, d)


def kernel(x, norm_mix_g, w_in, m_conv_w, m_conv_b, w_q, w_k, m_i_bias, m_f_bias, m_head_g, m_skip, s5_a_re, s5_a_im, s5_log_dt, s5_b_re, s5_b_im, s5_c_re, s5_c_im, s5_d, s5_w_glu, s5_b_glu, w_out, norm_ffn_g, w_gate, w_val, f_conv_w, f_conv_b, w_down, norm_final_g):
    return _trunk(x, norm_mix_g, w_in, m_conv_w, m_conv_b, w_q, w_k, m_i_bias, m_f_bias, m_head_g, m_skip,
                  s5_a_re, s5_a_im, s5_log_dt, s5_b_re, s5_b_im, s5_c_re, s5_c_im, s5_d, s5_w_glu, s5_b_glu,
                  w_out, norm_ffn_g, w_gate, w_val, f_conv_w, f_conv_b, w_down, norm_final_g)
```

```python
import functools
import math

import jax
import jax.numpy as jnp
from jax import lax
from jax.experimental import pallas as pl
from jax.experimental.pallas import tpu as pltpu

F32 = jnp.float32
BF16 = jnp.bfloat16
EPS = 1e-6

M_HEADS = 4
M_HEAD_V = 256
M_HEAD_QK = 128
M_WIDTH = M_HEADS * M_HEAD_V
M_CONV = 4
M_CHUNK = 64
S5_GROUP_CH = 16
S5_STATE = 64
S5_CH = 1024
S5_CHUNK = 8
F_CONV = 3
PROJ_U_COL = 3 * M_WIDTH
PROJ_GATE_COL = PROJ_U_COL + S5_CH
PROJ_COLS = 4608

LANES = 128
SUBLANES = 8
S5_SLAB_GROUPS = LANES // S5_GROUP_CH
S5_SLAB_STATE = S5_SLAB_GROUPS * S5_STATE
VMEM_LIMIT = 56 * 1024 * 1024


def _cparams(*sem):
    return pltpu.CompilerParams(dimension_semantics=sem, vmem_limit_bytes=VMEM_LIMIT)


def _rmsnorm_body(x_ref, g_ref, o_ref):
    x = x_ref[...]
    ms = jnp.mean(x * x, axis=-1, keepdims=True)
    o_ref[...] = (x * lax.rsqrt(ms + EPS) * g_ref[...]).astype(o_ref.dtype)


def _rmsnorm(x, g, out_dtype, tm=512):
    m, d = x.shape
    return pl.pallas_call(
        _rmsnorm_body,
        grid=(m // tm,),
        in_specs=[pl.BlockSpec((tm, d), lambda i: (i, 0)), pl.BlockSpec((1, d), lambda i: (0, 0))],
        out_specs=pl.BlockSpec((tm, d), lambda i: (i, 0)),
        out_shape=jax.ShapeDtypeStruct((m, d), out_dtype),
        compiler_params=_cparams("parallel"),
        name="rmsnorm",
    )(x, g.reshape(1, d))


def _in_proj_body(x_ref, g_ref, wh_ref, wt_ref, o_ref, h_ref, *, n_head):
    j = pl.program_id(1)

    @pl.when(j == 0)
    def _():
        x = x_ref[...]
        ms = jnp.mean(x * x, axis=-1, keepdims=True)
        h_ref[...] = (x * lax.rsqrt(ms + EPS) * g_ref[...]).astype(BF16)

    @pl.when(j < n_head)
    def _():
        o_ref[...] = jnp.dot(h_ref[...], wh_ref[...], preferred_element_type=F32)

    @pl.when(j >= n_head)
    def _():
        o_ref[...] = jnp.dot(h_ref[...], wt_ref[...], preferred_element_type=F32)


def _in_proj(x, g, w_head, w_tail, l, tm=1024, tn=768):
    m, k = x.shape
    n_head = PROJ_U_COL // tn
    n_tail = w_tail.shape[2] // tn
    return pl.pallas_call(
        functools.partial(_in_proj_body, n_head=n_head),
        grid=(m // tm, n_head + n_tail),
        in_specs=[
            pl.BlockSpec((tm, k), lambda i, j: (i, 0)),
            pl.BlockSpec((1, k), lambda i, j: (0, 0)),
            pl.BlockSpec((None, k, tn), lambda i, j: (l, 0, jnp.minimum(j, n_head - 1))),
            pl.BlockSpec((None, k, tn), lambda i, j: (l, 0, jnp.maximum(j - n_head, 0))),
        ],
        out_specs=pl.BlockSpec((tm, tn), lambda i, j: (i, j)),
        out_shape=jax.ShapeDtypeStruct((m, PROJ_U_COL + w_tail.shape[2]), F32),
        scratch_shapes=[pltpu.VMEM((tm, k), BF16)],
        compiler_params=_cparams("parallel", "arbitrary"),
        name="in_proj",
    )(x, g, w_head, w_tail)


def _log_sigmoid(x):
    return jnp.minimum(x, 0.0) - jnp.log1p(jnp.exp(-jnp.abs(x)))


def _mlstm_body(xm_ref, v_ref, o_ref, ig_ref, fg_ref, cw_ref, cb_ref, wq_ref, wk_ref, ib_ref, fb_ref,
                hg_ref, sk_ref, out_ref, ext_ref, c_ref, n_ref, m_ref, h_ref, *, tb):
    ncb = tb // M_CHUNK
    L = M_CHUNK

    @pl.when(pl.program_id(1) == 0)
    def _():
        ext_ref[0:SUBLANES, :] = jnp.zeros((SUBLANES, M_WIDTH), F32)
        c_ref[...] = jnp.zeros_like(c_ref)
        n_ref[...] = jnp.zeros_like(n_ref)
        m_ref[...] = jnp.zeros_like(m_ref)

    ext_ref[SUBLANES:, :] = xm_ref[...]
    acc = cb_ref[...] + cw_ref[0:1, :] * ext_ref[pl.ds(SUBLANES - 3, tb), :]
    for kk in range(1, M_CONV):
        acc = acc + cw_ref[kk:kk + 1, :] * ext_ref[pl.ds(SUBLANES - 3 + kk, tb), :]
    c = acc * jax.nn.sigmoid(acc)
    ext_ref[0:SUBLANES, :] = xm_ref[tb - SUBLANES:tb, :]

    i_pre = ig_ref[...] + ib_ref[...]
    lf = _log_sigmoid(fg_ref[...] + fb_ref[...])
    pos = lax.broadcasted_iota(jnp.int32, (tb, LANES), 0) % L
    bc = lf
    s = 1
    while s < L:
        bc = bc + jnp.where(pos >= s, pltpu.roll(bc, s, axis=0), 0.0)
        s *= 2
    rowv = i_pre - bc
    a = rowv
    s = 1
    while s < L:
        a = jnp.where(pos >= s, jnp.maximum(a, pltpu.roll(a, s, axis=0)), a)
        s *= 2
    bc3 = bc.reshape(ncb, L, LANES)
    g = bc3[:, L - 1:L, :]
    dec3 = g - bc3 + i_pre.reshape(ncb, L, LANES)
    maxdec = jnp.max(dec3, axis=1, keepdims=True)
    m_run = m_ref[0:1, :]
    m0_l, m1_l = [], []
    for nn in range(ncb):
        m0_l.append(m_run)
        m_run = jnp.maximum(g[nn] + m_run, maxdec[nn])
        m1_l.append(m_run)
    m_ref[0:1, :] = m_run
    m0 = jnp.stack(m0_l, axis=0)
    m1 = jnp.stack(m1_l, axis=0)
    mx = jnp.maximum(m0, a.reshape(ncb, L, LANES))
    colv = -mx
    w_inter = jnp.exp(m0 - mx)
    e_negm = jnp.exp(-(bc3 + mx))
    w_k = jnp.exp(dec3 - m1)
    w_c = jnp.exp(g + m0 - m1)
    rowv_t = rowv.T

    ii = lax.broadcasted_iota(jnp.int32, (L, 2 * L), 0)
    jj = lax.broadcasted_iota(jnp.int32, (L, 2 * L), 1)
    masks = [(jj <= ii), (jj >= L) & (jj - L <= ii)]
    scale = M_HEAD_QK ** -0.5

    for h in range(M_HEADS):
        c_h = c[:, h * M_HEAD_V:(h + 1) * M_HEAD_V].astype(BF16)
        q = jnp.dot(c_h, wq_ref[h], preferred_element_type=F32) * scale
        k = jnp.dot(c_h, wk_ref[h], preferred_element_type=F32)
        qb = q.astype(BF16)
        kb = k.astype(BF16)
        vb = v_ref[:, h * M_HEAD_V:(h + 1) * M_HEAD_V].astype(BF16)
        c_st = c_ref[h]
        n_st = n_ref[h]
        for nn in range(ncb):
            p0 = (nn // 2) * 2 * L
            r0 = nn * L
            q_n = q[r0:r0 + L]
            qb_n = qb[r0:r0 + L]
            sc = lax.dot_general(qb_n, kb[p0:p0 + 2 * L], (((1,), (1,)), ((), ())),
                                 preferred_element_type=F32)
            arg = colv[nn][:, h:h + 1] + rowv_t[h:h + 1, p0:p0 + 2 * L]
            sw = sc * jnp.exp(jnp.where(masks[nn % 2], arg, -jnp.inf))
            wi = w_inter[nn][:, h:h + 1]
            num = jnp.dot(sw.astype(BF16), vb[p0:p0 + 2 * L], preferred_element_type=F32)
            num = num + wi * jnp.dot(qb_n, c_st.astype(BF16), preferred_element_type=F32)
            den = jnp.sum(sw, axis=1, keepdims=True) + wi * jnp.sum(q_n * n_st, axis=1, keepdims=True)
            den = jnp.maximum(jnp.abs(den), e_negm[nn][:, h:h + 1])
            h_ref[r0:r0 + L, h * M_HEAD_V:(h + 1) * M_HEAD_V] = num / den
            kw = k[r0:r0 + L] * w_k[nn][:, h:h + 1]
            wc = w_c[nn][:, h:h + 1]
            upd = lax.dot_general(kw.astype(BF16), vb[r0:r0 + L], (((0,), (0,)), ((), ())),
                                  preferred_element_type=F32)
            c_st = wc * c_st + upd
            n_st = wc * n_st + jnp.sum(kw, axis=0, keepdims=True)
        c_ref[h] = c_st
        n_ref[h] = n_st

    for h in range(M_HEADS):
        sl = slice(h * M_HEAD_V, (h + 1) * M_HEAD_V)
        hh = h_ref[:, sl]
        hn = hh * lax.rsqrt(jnp.mean(hh * hh, axis=-1, keepdims=True) + EPS) * hg_ref[:, sl]
        out = jax.nn.sigmoid(o_ref[:, sl]) * (hn + sk_ref[:, sl] * c[:, sl])
        out_ref[:, sl] = out.astype(out_ref.dtype)


def _mlstm(proj, cw, cb, wq, wk, ib, fb, hg, sk, bsz, t, tb=256):
    nt = t // tb
    gate_blk = PROJ_GATE_COL // LANES
    row = lambda b, i: b * nt + i
    full = lambda shape: pl.BlockSpec(shape, lambda b, i: (0,) * len(shape))
    return pl.pallas_call(
        functools.partial(_mlstm_body, tb=tb),
        grid=(bsz, nt),
        in_specs=[
            pl.BlockSpec((tb, M_WIDTH), lambda b, i: (row(b, i), 0)),
            pl.BlockSpec((tb, M_WIDTH), lambda b, i: (row(b, i), 1)),
            pl.BlockSpec((tb, M_WIDTH), lambda b, i: (row(b, i), 2)),
            pl.BlockSpec((tb, LANES), lambda b, i: (row(b, i), gate_blk)),
            pl.BlockSpec((tb, LANES), lambda b, i: (row(b, i), gate_blk + 1)),
            full((M_CONV, M_WIDTH)), full((1, M_WIDTH)),
            full((M_HEADS, M_HEAD_V, M_HEAD_QK)), full((M_HEADS, M_HEAD_V, M_HEAD_QK)),
            full((1, LANES)), full((1, LANES)), full((1, M_WIDTH)), full((1, M_WIDTH)),
        ],
        out_specs=pl.BlockSpec((tb, M_WIDTH), lambda b, i: (row(b, i), 0)),
        out_shape=jax.ShapeDtypeStruct((bsz * t, M_WIDTH), BF16),
        scratch_shapes=[
            pltpu.VMEM((tb + SUBLANES, M_WIDTH), F32),
            pltpu.VMEM((M_HEADS, M_HEAD_QK, M_HEAD_V), F32),
            pltpu.VMEM((M_HEADS, 1, M_HEAD_QK), F32),
            pltpu.VMEM((SUBLANES, LANES), F32),
            pltpu.VMEM((tb, M_WIDTH), F32),
        ],
        compiler_params=_cparams("parallel", "arbitrary"),
        name="mlstm",
    )(proj, proj, proj, proj, proj, cw, cb, wq, wk, ib, fb, hg, sk)


def _s5_body(u_ref, pre_ref, pim_ref, bre_ref, bim_ref, cre_ref, cim_ref, ar_ref, ai_ref, br_ref, bi_ref,
             tr_ref, ti_ref, y_ref, wst_ref, wct_ref, bbr_ref, toe_ref, v_ref, s_ref, yall_ref,
             *, nl, nsg, tseg):
    lc = S5_CHUNK
    ns = S5_SLAB_STATE
    gch = S5_GROUP_CH
    nseq = SUBLANES

    @pl.when(pl.program_id(0) == 0)
    def _():
        wst_ref[...] = jnp.zeros_like(wst_ref)
        wct_ref[...] = jnp.zeros_like(wct_ref)
        bbr_ref[...] = jnp.zeros_like(bbr_ref)
        toe_ref[...] = jnp.zeros_like(toe_ref)

    for gg in range(S5_SLAB_GROUPS):
        k0 = (gg // 2) * LANES
        bre, bim, cre, cim = bre_ref[gg], bim_ref[gg], cre_ref[gg], cim_ref[gg]
        r0 = gg * gch
        bbr_ref[r0:r0 + gch, k0:k0 + LANES] = bre.astype(BF16)
        bbr_ref[r0:r0 + gch, ns + k0:ns + k0 + LANES] = bim.astype(BF16)
        for j in range(lc + 1):
            p_re = pre_ref[j, gg:gg + 1, :]
            p_im = pim_ref[j, gg:gg + 1, :]
            r1 = j * LANES + r0
            wct_ref[r1:r1 + gch, k0:k0 + LANES] = (cre * p_re - cim * p_im).astype(BF16)
            wct_ref[r1:r1 + gch, ns + k0:ns + k0 + LANES] = (-(cre * p_im + cim * p_re)).astype(BF16)
            if j < lc:
                r2 = (lc - 1 - j) * LANES + r0
                wst_ref[r2:r2 + gch, k0:k0 + LANES] = (bre * p_re - bim * p_im).astype(BF16)
                wst_ref[r2:r2 + gch, ns + k0:ns + k0 + LANES] = (bre * p_im + bim * p_re).astype(BF16)
    kt = lax.dot_general(bbr_ref[...], wct_ref[0:lc * LANES, :], (((1,), (1,)), ((), ())),
                         preferred_element_type=F32).astype(BF16)
    for i in range(lc):
        toe_ref[i * LANES:(i + 1) * LANES, i * LANES:] = kt[:, :(lc - i) * LANES]

    ut = jnp.swapaxes(u_ref[...].reshape(nseq, tseg, LANES), 0, 1).reshape(nl, lc, nseq, LANES)
    uc = jnp.concatenate([ut[:, i] for i in range(lc)], axis=-1).reshape(nl * nseq, lc * LANES).astype(BF16)
    yall_ref[...] = jnp.dot(uc, toe_ref[...], preferred_element_type=F32)
    v_ref[...] = jnp.dot(uc, wst_ref[...], preferred_element_type=F32)

    ar = jnp.broadcast_to(ar_ref[...], (nseq, ns))
    ai = jnp.broadcast_to(ai_ref[...], (nseq, ns))

    def scan_step(n, carry):
        s_re, s_im = carry
        r0 = pl.multiple_of(n * nseq, nseq)
        s_ref[pl.ds(r0, nseq), 0:ns] = s_re
        s_ref[pl.ds(r0, nseq), ns:] = s_im
        n_re = ar * s_re - ai * s_im + v_ref[pl.ds(r0, nseq), 0:ns]
        n_im = ar * s_im + ai * s_re + v_ref[pl.ds(r0, nseq), ns:]
        return n_re, n_im

    zero = jnp.zeros((nseq, ns), F32)
    f_re, f_im = lax.fori_loop(0, nl, scan_step, (zero, zero))

    br = jnp.broadcast_to(br_ref[...], (nseq, ns))
    bi = jnp.broadcast_to(bi_ref[...], (nseq, ns))
    has_prev = lax.broadcasted_iota(jnp.int32, (nseq, ns), 0) % nsg >= 1
    i_re, i_im = zero, zero
    for _ in range(nsg - 1):
        t_re = f_re + br * i_re - bi * i_im
        t_im = f_im + br * i_im + bi * i_re
        i_re = jnp.where(has_prev, pltpu.roll(t_re, 1, axis=0), 0.0)
        i_im = jnp.where(has_prev, pltpu.roll(t_im, 1, axis=0), 0.0)

    def fix_step(n, _):
        r0 = pl.multiple_of(n * nseq, nseq)
        p_re = tr_ref[pl.ds(n, 1), :]
        p_im = ti_ref[pl.ds(n, 1), :]
        s_ref[pl.ds(r0, nseq), 0:ns] += p_re * i_re - p_im * i_im
        s_ref[pl.ds(r0, nseq), ns:] += p_re * i_im + p_im * i_re
        return 0

    lax.fori_loop(0, nl, fix_step, 0)

    ya = yall_ref[...] + lax.dot_general(s_ref[...].astype(BF16), wct_ref[LANES:(lc + 1) * LANES, :],
                                         (((1,), (1,)), ((), ())), preferred_element_type=F32)
    ya = ya.reshape(nl, nseq, lc * LANES)
    yt = jnp.stack([ya[:, :, j * LANES:(j + 1) * LANES] for j in range(lc)], axis=1)
    y_ref[...] = jnp.swapaxes(yt.reshape(tseg, nseq, LANES), 0, 1).reshape(nseq * tseg, LANES)


def _s5(proj, prm, bsz, t):
    bt = bsz * t
    nsg = SUBLANES // bsz
    tseg = t // nsg
    nl = tseg // S5_CHUNK
    r = nl * SUBLANES
    lc = S5_CHUNK
    ns = S5_SLAB_STATE
    n_slab = prm["bb_re"].shape[0] // S5_SLAB_GROUPS
    u_blk0 = PROJ_U_COL // LANES
    grp = lambda shape: pl.BlockSpec((S5_SLAB_GROUPS,) + shape, lambda s: (s,) + (0,) * len(shape))
    vec = lambda rows: pl.BlockSpec((rows, ns), lambda s: (0, s))
    return pl.pallas_call(
        functools.partial(_s5_body, nl=nl, nsg=nsg, tseg=tseg),
        grid=(n_slab,),
        in_specs=[
            pl.BlockSpec((bt, LANES), lambda s: (0, u_blk0 + s)),
            pl.BlockSpec((lc + 1, S5_SLAB_GROUPS, LANES), lambda s: (0, s, 0)),
            pl.BlockSpec((lc + 1, S5_SLAB_GROUPS, LANES), lambda s: (0, s, 0)),
            grp((S5_GROUP_CH, LANES)), grp((S5_GROUP_CH, LANES)), grp((S5_GROUP_CH, LANES)), grp((S5_GROUP_CH, LANES)),
            vec(1), vec(1), vec(1), vec(1), vec(nl), vec(nl),
        ],
        out_specs=pl.BlockSpec((bt, LANES), lambda s: (0, s)),
        out_shape=jax.ShapeDtypeStruct((bt, n_slab * LANES), F32),
        scratch_shapes=[
            pltpu.VMEM((lc * LANES, 2 * ns), BF16),
            pltpu.VMEM(((lc + 1) * LANES, 2 * ns), BF16),
            pltpu.VMEM((LANES, 2 * ns), BF16),
            pltpu.VMEM((lc * LANES, lc * LANES), BF16),
            pltpu.VMEM((r, 2 * ns), F32),
            pltpu.VMEM((r, 2 * ns), F32),
            pltpu.VMEM((r, lc * LANES), F32),
        ],
        compiler_params=_cparams("arbitrary"),
        name="s5",
    )(proj, prm["pw_re"], prm["pw_im"], prm["bb_re"], prm["bb_im"], prm["c_re"], prm["c_im"],
      prm["ac_re"], prm["ac_im"], prm["as_re"], prm["as_im"], prm["pt_re"], prm["pt_im"])


def _s5_params(a_re, a_im, log_dt, b_re, b_im, c_re, c_im, nl):
    lc = S5_CHUNK
    nlay, ng, npst = a_re.shape
    dt = jnp.exp(log_dt)[..., None]
    lam_re, lam_im = a_re * dt, a_im * dt

    def apow(n):
        n = n.astype(F32)[None, :, None, None]
        mag = jnp.exp(lam_re[:, None] * n)
        return mag * jnp.cos(lam_im[:, None] * n), mag * jnp.sin(lam_im[:, None] * n)

    ab_re, ab_im = jnp.exp(lam_re) * jnp.cos(lam_im), jnp.exp(lam_re) * jnp.sin(lam_im)
    den = a_re * a_re + a_im * a_im
    f_re = ((ab_re - 1.0) * a_re + ab_im * a_im) / den
    f_im = (ab_im * a_re - (ab_re - 1.0) * a_im) / den
    bb_re = (f_re[..., None] * b_re - f_im[..., None] * b_im).transpose(0, 1, 3, 2)
    bb_im = (f_re[..., None] * b_im + f_im[..., None] * b_re).transpose(0, 1, 3, 2)
    odd = (jnp.arange(ng) % 2 == 1)[None, :, None, None]

    def half(x):
        z = jnp.zeros_like(x)
        return jnp.concatenate([jnp.where(odd, z, x), jnp.where(odd, x, z)], axis=-1)

    both = lambda x: jnp.concatenate([x, x], axis=-1)
    flat = lambda x: x.reshape(nlay, x.shape[1], ng * npst)
    pw_re, pw_im = apow(jnp.arange(lc + 1))
    ac_re, ac_im = apow(jnp.array([lc]))
    as_re, as_im = apow(jnp.array([lc * nl]))
    pt_re, pt_im = apow(jnp.arange(nl) * lc)
    return {
        "pw_re": both(pw_re), "pw_im": both(pw_im),
        "bb_re": half(bb_re), "bb_im": half(bb_im), "c_re": half(c_re), "c_im": half(c_im),
        "ac_re": flat(ac_re), "ac_im": flat(ac_im), "as_re": flat(as_re), "as_im": flat(as_im),
        "pt_re": flat(pt_re), "pt_im": flat(pt_im),
    }


def _glu_body(y_ref, u_ref, d_ref, w_ref, b_ref, o_ref):
    y = jax.nn.gelu(y_ref[...] + d_ref[...] * u_ref[...])
    z = jnp.dot(y.astype(BF16), w_ref[...], preferred_element_type=F32) + b_ref[...]
    o_ref[...] = (y * jax.nn.sigmoid(z)).astype(o_ref.dtype)


def _glu(y, proj, d, w, l, b, tm=512):
    m, n = y.shape
    u_blk = PROJ_U_COL // n
    return pl.pallas_call(
        _glu_body,
        grid=(m // tm,),
        in_specs=[
            pl.BlockSpec((tm, n), lambda i: (i, 0)),
            pl.BlockSpec((tm, n), lambda i: (i, u_blk)),
            pl.BlockSpec((1, n), lambda i: (0, 0)),
            pl.BlockSpec((None, n, n), lambda i: (l, 0, 0)),
            pl.BlockSpec((1, n), lambda i: (0, 0)),
        ],
        out_specs=pl.BlockSpec((tm, n), lambda i: (i, 0)),
        out_shape=jax.ShapeDtypeStruct((m, n), BF16),
        compiler_params=_cparams("parallel"),
        name="s5_glu",
    )(y, proj, d, w, b)


def _outproj_body(m_ref, s_ref, wm_ref, ws_ref, x_ref, g_ref, xo_ref, ho_ref):
    acc = jnp.dot(m_ref[...], wm_ref[...], preferred_element_type=F32)
    acc = acc + jnp.dot(s_ref[...], ws_ref[...], preferred_element_type=F32)
    x = x_ref[...] + acc
    xo_ref[...] = x
    ms = jnp.mean(x * x, axis=-1, keepdims=True)
    ho_ref[...] = (x * lax.rsqrt(ms + EPS) * g_ref[...]).astype(ho_ref.dtype)


def _outproj(m_out, s_out, w, l, x, g, tm=512):
    m, d = x.shape
    kh = m_out.shape[1]
    return pl.pallas_call(
        _outproj_body,
        grid=(m // tm,),
        in_specs=[
            pl.BlockSpec((tm, kh), lambda i: (i, 0)),
            pl.BlockSpec((tm, kh), lambda i: (i, 0)),
            pl.BlockSpec((None, kh, d), lambda i: (l, 0, 0)),
            pl.BlockSpec((None, kh, d), lambda i: (l, 1, 0)),
            pl.BlockSpec((tm, d), lambda i: (i, 0)),
            pl.BlockSpec((1, d), lambda i: (0, 0)),
        ],
        out_specs=[pl.BlockSpec((tm, d), lambda i: (i, 0)), pl.BlockSpec((tm, d), lambda i: (i, 0))],
        out_shape=[jax.ShapeDtypeStruct((m, d), F32), jax.ShapeDtypeStruct((m, d), BF16)],
        compiler_params=_cparams("parallel"),
        name="out_proj",
    )(m_out, s_out, w, w, x, g)


def _ffn_up_body(h_ref, wg_ref, wv_ref, cw_ref, cb_ref, a_ref, wgb_ref, wvb_ref, ext_ref, *, tm, ts, tiles_per_seq):
    i = pl.program_id(1)

    @pl.when(i == 0)
    def _():
        wgb_ref[...] = wg_ref[...].astype(BF16)
        wvb_ref[...] = wv_ref[...].astype(BF16)

    @pl.when(i % tiles_per_seq == 0)
    def _():
        ext_ref[0:SUBLANES, :] = jnp.zeros((SUBLANES, ext_ref.shape[1]), F32)

    for r0 in range(0, tm, ts):
        hs = h_ref[r0:r0 + ts, :]
        g = jnp.dot(hs, wgb_ref[...], preferred_element_type=F32)
        ext_ref[SUBLANES + r0:SUBLANES + r0 + ts, :] = g
        conv = cb_ref[...] + cw_ref[F_CONV - 1:F_CONV, :] * g
        for kk in range(F_CONV - 1):
            conv = conv + cw_ref[kk:kk + 1, :] * ext_ref[pl.ds(SUBLANES + r0 - (F_CONV - 1) + kk, ts), :]
        val = jnp.dot(hs, wvb_ref[...], preferred_element_type=F32)
        a_ref[r0:r0 + ts, :] = (jax.nn.gelu(conv) * val).astype(a_ref.dtype)
    ext_ref[0:SUBLANES, :] = ext_ref[tm:tm + SUBLANES, :]


def _ffn_up(h, wg, wv, cw, cb, l, t, tm=2048, tn=512, ts=256):
    m, k = h.shape
    n = wg.shape[2]
    tm = min(tm, t)
    return pl.pallas_call(
        functools.partial(_ffn_up_body, tm=tm, ts=ts, tiles_per_seq=t // tm),
        grid=(pl.cdiv(n, tn), m // tm),
        in_specs=[
            pl.BlockSpec((tm, k), lambda j, i: (i, 0)),
            pl.BlockSpec((None, k, tn), lambda j, i: (l, 0, j)),
            pl.BlockSpec((None, k, tn), lambda j, i: (l, 0, j)),
            pl.BlockSpec((None, F_CONV, tn), lambda j, i: (l, 0, j)),
            pl.BlockSpec((None, 1, tn), lambda j, i: (l, 0, j)),
        ],
        out_specs=pl.BlockSpec((tm, tn), lambda j, i: (i, j)),
        out_shape=jax.ShapeDtypeStruct((m, n), BF16),
        scratch_shapes=[
            pltpu.VMEM((k, tn), BF16), pltpu.VMEM((k, tn), BF16),
            pltpu.VMEM((tm + SUBLANES, tn), F32),
        ],
        compiler_params=_cparams("parallel", "arbitrary"),
        name="ffn_up",
    )(h, wg, wv, cw, cb)


def _ffn_down_body(a_ref, w_ref, x_ref, xo_ref, wb_ref):
    @pl.when(pl.program_id(1) == 0)
    def _():
        wb_ref[...] = w_ref[...].astype(BF16)

    xo_ref[...] = x_ref[...] + jnp.dot(a_ref[...], wb_ref[...], preferred_element_type=F32)


def _ffn_down(a, w, l, x, tm=512, tn=512):
    m, d = x.shape
    k = a.shape[1]
    return pl.pallas_call(
        _ffn_down_body,
        grid=(d // tn, m // tm),
        in_specs=[
            pl.BlockSpec((tm, k), lambda j, i: (i, 0)),
            pl.BlockSpec((None, k, tn), lambda j, i: (l, 0, j)),
            pl.BlockSpec((tm, tn), lambda j, i: (i, j)),
        ],
        out_specs=pl.BlockSpec((tm, tn), lambda j, i: (i, j)),
        out_shape=jax.ShapeDtypeStruct((m, d), F32),
        scratch_shapes=[pltpu.VMEM((k, tn), BF16)],
        compiler_params=_cparams("parallel", "arbitrary"),
        name="ffn_down",
    )(a, w, x)


def _pad_cols(w, mult):
    pad = (-w.shape[-1]) % mult
    return jnp.pad(w, [(0, 0)] * (w.ndim - 1) + [(0, pad)])


def _trunk(x, norm_mix_g, w_in, m_conv_w, m_conv_b, w_q, w_k, m_i_bias, m_f_bias, m_head_g, m_skip,
           s5_a_re, s5_a_im, s5_log_dt, s5_b_re, s5_b_im, s5_c_re, s5_c_im, s5_d, s5_w_glu, s5_b_glu,
           w_out, norm_ffn_g, w_gate, w_val, f_conv_w, f_conv_b, w_down, norm_final_g):
    bsz, t, d = x.shape
    depth = w_in.shape[0]
    bt = bsz * t
    nl = t // (SUBLANES // bsz) // S5_CHUNK

    c3 = PROJ_U_COL
    w_tail = jnp.concatenate([
        w_in[:, :, c3 + 2 * M_HEADS:],
        _pad_cols(w_in[:, :, c3:c3 + M_HEADS], LANES), _pad_cols(w_in[:, :, c3 + M_HEADS:c3 + 2 * M_HEADS], LANES),
    ], axis=-1)
    w_tail = _pad_cols(w_tail, PROJ_COLS - c3).astype(BF16)
    w_head = w_in[:, :, :c3].astype(BF16)
    ib = _pad_cols(m_i_bias, LANES)[:, None, :]
    fb = _pad_cols(m_f_bias, LANES)[:, None, :]
    wq_b, wk_b = w_q.astype(BF16), w_k.astype(BF16)
    s5p = _s5_params(s5_a_re, s5_a_im, s5_log_dt, s5_b_re, s5_b_im, s5_c_re, s5_c_im, nl)
    wglu_b = s5_w_glu.astype(BF16)
    wout_b = w_out.astype(BF16)

    xf = x.reshape(bt, d)
    for l in range(depth):
        proj = _in_proj(xf, norm_mix_g[l][None], w_head, w_tail, l)
        m_out = _mlstm(proj, m_conv_w[l], m_conv_b[l][None], wq_b[l], wk_b[l], ib[l], fb[l],
                       m_head_g[l][None], m_skip[l][None], bsz, t)
        y = _s5(proj, {k_: v_[l] for k_, v_ in s5p.items()}, bsz, t)
        s_out = _glu(y, proj, s5_d[l][None], wglu_b, l, s5_b_glu[l][None])
        xf, h = _outproj(m_out, s_out, wout_b, l, xf, norm_ffn_g[l][None])
        a = _ffn_up(h, w_gate, w_val, f_conv_w, f_conv_b[:, None, :], l, t)
        xf = _ffn_down(a, w_down, l, xf)
    return _rmsnorm(xf, norm_final_g, F32).reshape(bsz, t, d)


def kernel(x, norm_mix_g, w_in, m_conv_w, m_conv_b, w_q, w_k, m_i_bias, m_f_bias, m_head_g, m_skip, s5_a_re, s5_a_im, s5_log_dt, s5_b_re, s5_b_im, s5_c_re, s5_c_im, s5_d, s5_w_glu, s5_b_glu, w_out, norm_ffn_g, w_gate, w_val, f_conv_w, f_conv_b, w_down, norm_final_g):
    return _trunk(x, norm_mix_g, w_in, m_conv_w, m_conv_b, w_q, w_k, m_i_bias, m_f_bias, m_head_g, m_skip,
                  s5_a_re, s5_a_im, s5_log_dt, s5_b_re, s5_b_im, s5_c_re, s5_c_im, s5_d, s5_w_glu, s5_b_glu,
                  w_out, norm_ffn_g, w_gate, w_val, f_conv_w, f_conv_b, w_down, norm_final_g)
```

```python
import functools
import math

import jax
import jax.numpy as jnp
from jax import lax
from jax.experimental import pallas as pl
from jax.experimental.pallas import tpu as pltpu

F32 = jnp.float32
BF16 = jnp.bfloat16
EPS = 1e-6

M_HEADS = 4
M_HEAD_V = 256
M_HEAD_QK = 128
M_WIDTH = M_HEADS * M_HEAD_V
M_CONV = 4
M_CHUNK = 128
S5_GROUP_CH = 16
S5_STATE = 64
S5_CH = 1024
S5_CHUNK = 8
F_CONV = 3
PROJ_U_COL = 3 * M_WIDTH
PROJ_GATE_COL = PROJ_U_COL + S5_CH
PROJ_COLS = 4608

LANES = 128
SUBLANES = 8
S5_SLAB_GROUPS = LANES // S5_GROUP_CH
S5_SLAB_STATE = S5_SLAB_GROUPS * S5_STATE
VMEM_LIMIT = 56 * 1024 * 1024


def _cparams(*sem):
    return pltpu.CompilerParams(dimension_semantics=sem, vmem_limit_bytes=VMEM_LIMIT)


def _rmsnorm_body(x_ref, g_ref, o_ref):
    x = x_ref[...]
    ms = jnp.mean(x * x, axis=-1, keepdims=True)
    o_ref[...] = (x * lax.rsqrt(ms + EPS) * g_ref[...]).astype(o_ref.dtype)


def _rmsnorm(x, g, out_dtype, tm=512):
    m, d = x.shape
    return pl.pallas_call(
        _rmsnorm_body,
        grid=(m // tm,),
        in_specs=[pl.BlockSpec((tm, d), lambda i: (i, 0)), pl.BlockSpec((1, d), lambda i: (0, 0))],
        out_specs=pl.BlockSpec((tm, d), lambda i: (i, 0)),
        out_shape=jax.ShapeDtypeStruct((m, d), out_dtype),
        compiler_params=_cparams("parallel"),
        name="rmsnorm",
    )(x, g.reshape(1, d))


def _in_proj_body(x_ref, g_ref, wh_ref, wt_ref, o_ref, h_ref, *, n_head):
    j = pl.program_id(1)

    @pl.when(j == 0)
    def _():
        x = x_ref[...]
        ms = jnp.mean(x * x, axis=-1, keepdims=True)
        h_ref[...] = (x * lax.rsqrt(ms + EPS) * g_ref[...]).astype(BF16)

    @pl.when(j < n_head)
    def _():
        o_ref[...] = jnp.dot(h_ref[...], wh_ref[...], preferred_element_type=F32)

    @pl.when(j >= n_head)
    def _():
        o_ref[...] = jnp.dot(h_ref[...], wt_ref[...], preferred_element_type=F32)


def _in_proj(x, g, w_head, w_tail, l, tm=1024, tn=768):
    m, k = x.shape
    n_head = PROJ_U_COL // tn
    n_tail = w_tail.shape[2] // tn
    return pl.pallas_call(
        functools.partial(_in_proj_body, n_head=n_head),
        grid=(m // tm, n_head + n_tail),
        in_specs=[
            pl.BlockSpec((tm, k), lambda i, j: (i, 0)),
            pl.BlockSpec((1, k), lambda i, j: (0, 0)),
            pl.BlockSpec((None, k, tn), lambda i, j: (l, 0, jnp.minimum(j, n_head - 1))),
            pl.BlockSpec((None, k, tn), lambda i, j: (l, 0, jnp.maximum(j - n_head, 0))),
        ],
        out_specs=pl.BlockSpec((tm, tn), lambda i, j: (i, j)),
        out_shape=jax.ShapeDtypeStruct((m, PROJ_U_COL + w_tail.shape[2]), F32),
        scratch_shapes=[pltpu.VMEM((tm, k), BF16)],
        compiler_params=_cparams("parallel", "arbitrary"),
        name="in_proj",
    )(x, g, w_head, w_tail)


def _log_sigmoid(x):
    return jnp.minimum(x, 0.0) - jnp.log1p(jnp.exp(-jnp.abs(x)))


def _mlstm_body(xm_ref, v_ref, o_ref, ig_ref, fg_ref, cw_ref, cb_ref, wq_ref, wk_ref, ib_ref, fb_ref,
                hg_ref, sk_ref, out_ref, ext_ref, c_ref, n_ref, m_ref, h_ref, *, tb):
    ncb = tb // M_CHUNK
    L = M_CHUNK

    @pl.when(pl.program_id(1) == 0)
    def _():
        ext_ref[0:SUBLANES, :] = jnp.zeros((SUBLANES, M_WIDTH), F32)
        c_ref[...] = jnp.zeros_like(c_ref)
        n_ref[...] = jnp.zeros_like(n_ref)
        m_ref[...] = jnp.zeros_like(m_ref)

    ext_ref[SUBLANES:, :] = xm_ref[...]
    acc = cb_ref[...] + cw_ref[0:1, :] * ext_ref[pl.ds(SUBLANES - 3, tb), :]
    for kk in range(1, M_CONV):
        acc = acc + cw_ref[kk:kk + 1, :] * ext_ref[pl.ds(SUBLANES - 3 + kk, tb), :]
    c = acc * jax.nn.sigmoid(acc)
    ext_ref[0:SUBLANES, :] = xm_ref[tb - SUBLANES:tb, :]

    i_pre = ig_ref[...] + ib_ref[...]
    lf = _log_sigmoid(fg_ref[...] + fb_ref[...])
    pos = lax.broadcasted_iota(jnp.int32, (tb, LANES), 0) % L
    bc = lf
    s = 1
    while s < L:
        bc = bc + jnp.where(pos >= s, pltpu.roll(bc, s, axis=0), 0.0)
        s *= 2
    rowv = i_pre - bc
    a = rowv
    s = 1
    while s < L:
        a = jnp.where(pos >= s, jnp.maximum(a, pltpu.roll(a, s, axis=0)), a)
        s *= 2
    bc3 = bc.reshape(ncb, L, LANES)
    g = bc3[:, L - 1:L, :]
    dec3 = g - bc3 + i_pre.reshape(ncb, L, LANES)
    maxdec = jnp.max(dec3, axis=1, keepdims=True)
    m_run = m_ref[0:1, :]
    m0_l, m1_l = [], []
    for nn in range(ncb):
        m0_l.append(m_run)
        m_run = jnp.maximum(g[nn] + m_run, maxdec[nn])
        m1_l.append(m_run)
    m_ref[0:1, :] = m_run
    m0 = jnp.stack(m0_l, axis=0)
    m1 = jnp.stack(m1_l, axis=0)
    mx = jnp.maximum(m0, a.reshape(ncb, L, LANES))
    colv = -mx
    w_inter = jnp.exp(m0 - mx)
    e_negm = jnp.exp(-(bc3 + mx))
    w_k = jnp.exp(dec3 - m1)
    w_c = jnp.exp(g + m0 - m1)
    rowv_t = rowv.T

    causal = lax.broadcasted_iota(jnp.int32, (L, L), 1) <= lax.broadcasted_iota(jnp.int32, (L, L), 0)
    scale = M_HEAD_QK ** -0.5

    for h in range(M_HEADS):
        c_h = c[:, h * M_HEAD_V:(h + 1) * M_HEAD_V].astype(BF16)
        q = jnp.dot(c_h, wq_ref[h], preferred_element_type=F32) * scale
        k = jnp.dot(c_h, wk_ref[h], preferred_element_type=F32)
        qb = q.astype(BF16)
        kb = k.astype(BF16)
        vb = v_ref[:, h * M_HEAD_V:(h + 1) * M_HEAD_V].astype(BF16)
        c_st = c_ref[h]
        n_st = n_ref[h]
        for nn in range(ncb):
            r0 = nn * L
            q_n = q[r0:r0 + L]
            qb_n = qb[r0:r0 + L]
            sc = lax.dot_general(qb_n, kb[r0:r0 + L], (((1,), (1,)), ((), ())),
                                 preferred_element_type=F32)
            arg = colv[nn][:, h:h + 1] + rowv_t[h:h + 1, r0:r0 + L]
            sw = sc * jnp.exp(jnp.where(causal, arg, -jnp.inf))
            wi = w_inter[nn][:, h:h + 1]
            num = jnp.dot(sw.astype(BF16), vb[r0:r0 + L], preferred_element_type=F32)
            num = num + wi * jnp.dot(qb_n, c_st.astype(BF16), preferred_element_type=F32)
            den = jnp.sum(sw, axis=1, keepdims=True) + wi * jnp.sum(q_n * n_st, axis=1, keepdims=True)
            den = jnp.maximum(jnp.abs(den), e_negm[nn][:, h:h + 1])
            h_ref[r0:r0 + L, h * M_HEAD_V:(h + 1) * M_HEAD_V] = num / den
            kw = k[r0:r0 + L] * w_k[nn][:, h:h + 1]
            wc = w_c[nn][:, h:h + 1]
            upd = lax.dot_general(kw.astype(BF16), vb[r0:r0 + L], (((0,), (0,)), ((), ())),
                                  preferred_element_type=F32)
            c_st = wc * c_st + upd
            n_st = wc * n_st + jnp.sum(kw, axis=0, keepdims=True)
        c_ref[h] = c_st
        n_ref[h] = n_st

    for h in range(M_HEADS):
        sl = slice(h * M_HEAD_V, (h + 1) * M_HEAD_V)
        hh = h_ref[:, sl]
        hn = hh * lax.rsqrt(jnp.mean(hh * hh, axis=-1, keepdims=True) + EPS) * hg_ref[:, sl]
        out = jax.nn.sigmoid(o_ref[:, sl]) * (hn + sk_ref[:, sl] * c[:, sl])
        out_ref[:, sl] = out.astype(out_ref.dtype)


def _mlstm(proj, cw, cb, wq, wk, ib, fb, hg, sk, bsz, t, tb=256):
    nt = t // tb
    gate_blk = PROJ_GATE_COL // LANES
    row = lambda b, i: b * nt + i
    full = lambda shape: pl.BlockSpec(shape, lambda b, i: (0,) * len(shape))
    return pl.pallas_call(
        functools.partial(_mlstm_body, tb=tb),
        grid=(bsz, nt),
        in_specs=[
            pl.BlockSpec((tb, M_WIDTH), lambda b, i: (row(b, i), 0)),
            pl.BlockSpec((tb, M_WIDTH), lambda b, i: (row(b, i), 1)),
            pl.BlockSpec((tb, M_WIDTH), lambda b, i: (row(b, i), 2)),
            pl.BlockSpec((tb, LANES), lambda b, i: (row(b, i), gate_blk)),
            pl.BlockSpec((tb, LANES), lambda b, i: (row(b, i), gate_blk + 1)),
            full((M_CONV, M_WIDTH)), full((1, M_WIDTH)),
            full((M_HEADS, M_HEAD_V, M_HEAD_QK)), full((M_HEADS, M_HEAD_V, M_HEAD_QK)),
            full((1, LANES)), full((1, LANES)), full((1, M_WIDTH)), full((1, M_WIDTH)),
        ],
        out_specs=pl.BlockSpec((tb, M_WIDTH), lambda b, i: (row(b, i), 0)),
        out_shape=jax.ShapeDtypeStruct((bsz * t, M_WIDTH), BF16),
        scratch_shapes=[
            pltpu.VMEM((tb + SUBLANES, M_WIDTH), F32),
            pltpu.VMEM((M_HEADS, M_HEAD_QK, M_HEAD_V), F32),
            pltpu.VMEM((M_HEADS, 1, M_HEAD_QK), F32),
            pltpu.VMEM((SUBLANES, LANES), F32),
            pltpu.VMEM((tb, M_WIDTH), F32),
        ],
        compiler_params=_cparams("parallel", "arbitrary"),
        name="mlstm",
    )(proj, proj, proj, proj, proj, cw, cb, wq, wk, ib, fb, hg, sk)


def _s5_body(u_ref, pre_ref, pim_ref, bre_ref, bim_ref, cre_ref, cim_ref, ar_ref, ai_ref, br_ref, bi_ref,
             tr_ref, ti_ref, y_ref, wst_ref, wct_ref, bbr_ref, toe_ref, v_ref, s_ref, yall_ref,
             *, nl, nsg, tseg):
    lc = S5_CHUNK
    ns = S5_SLAB_STATE
    gch = S5_GROUP_CH
    nseq = SUBLANES

    @pl.when(pl.program_id(0) == 0)
    def _():
        wst_ref[...] = jnp.zeros_like(wst_ref)
        wct_ref[...] = jnp.zeros_like(wct_ref)
        bbr_ref[...] = jnp.zeros_like(bbr_ref)
        toe_ref[...] = jnp.zeros_like(toe_ref)

    for gg in range(S5_SLAB_GROUPS):
        k0 = (gg // 2) * LANES
        bre, bim, cre, cim = bre_ref[gg], bim_ref[gg], cre_ref[gg], cim_ref[gg]
        r0 = gg * gch
        bbr_ref[r0:r0 + gch, k0:k0 + LANES] = bre.astype(BF16)
        bbr_ref[r0:r0 + gch, ns + k0:ns + k0 + LANES] = bim.astype(BF16)
        for j in range(lc + 1):
            p_re = pre_ref[j, gg:gg + 1, :]
            p_im = pim_ref[j, gg:gg + 1, :]
            r1 = j * LANES + r0
            wct_ref[r1:r1 + gch, k0:k0 + LANES] = (cre * p_re - cim * p_im).astype(BF16)
            wct_ref[r1:r1 + gch, ns + k0:ns + k0 + LANES] = (-(cre * p_im + cim * p_re)).astype(BF16)
            if j < lc:
                r2 = (lc - 1 - j) * LANES + r0
                wst_ref[r2:r2 + gch, k0:k0 + LANES] = (bre * p_re - bim * p_im).astype(BF16)
                wst_ref[r2:r2 + gch, ns + k0:ns + k0 + LANES] = (bre * p_im + bim * p_re).astype(BF16)
    kt = lax.dot_general(bbr_ref[...], wct_ref[0:lc * LANES, :], (((1,), (1,)), ((), ())),
                         preferred_element_type=F32).astype(BF16)
    for i in range(lc):
        toe_ref[i * LANES:(i + 1) * LANES, i * LANES:] = kt[:, :(lc - i) * LANES]

    ut = jnp.swapaxes(u_ref[...].reshape(nseq, tseg, LANES), 0, 1).reshape(nl, lc, nseq, LANES)
    uc = jnp.concatenate([ut[:, i] for i in range(lc)], axis=-1).reshape(nl * nseq, lc * LANES).astype(BF16)
    yall_ref[...] = jnp.dot(uc, toe_ref[...], preferred_element_type=F32)
    v_ref[...] = jnp.dot(uc, wst_ref[...], preferred_element_type=F32)

    ar = jnp.broadcast_to(ar_ref[...], (nseq, ns))
    ai = jnp.broadcast_to(ai_ref[...], (nseq, ns))

    def scan_step(n, carry):
        s_re, s_im = carry
        r0 = pl.multiple_of(n * nseq, nseq)
        s_ref[pl.ds(r0, nseq), 0:ns] = s_re
        s_ref[pl.ds(r0, nseq), ns:] = s_im
        n_re = ar * s_re - ai * s_im + v_ref[pl.ds(r0, nseq), 0:ns]
        n_im = ar * s_im + ai * s_re + v_ref[pl.ds(r0, nseq), ns:]
        return n_re, n_im

    zero = jnp.zeros((nseq, ns), F32)
    f_re, f_im = lax.fori_loop(0, nl, scan_step, (zero, zero))

    br = jnp.broadcast_to(br_ref[...], (nseq, ns))
    bi = jnp.broadcast_to(bi_ref[...], (nseq, ns))
    has_prev = lax.broadcasted_iota(jnp.int32, (nseq, ns), 0) % nsg >= 1
    i_re, i_im = zero, zero
    for _ in range(nsg - 1):
        t_re = f_re + br * i_re - bi * i_im
        t_im = f_im + br * i_im + bi * i_re
        i_re = jnp.where(has_prev, pltpu.roll(t_re, 1, axis=0), 0.0)
        i_im = jnp.where(has_prev, pltpu.roll(t_im, 1, axis=0), 0.0)

    def fix_step(n, _):
        r0 = pl.multiple_of(n * nseq, nseq)
        p_re = tr_ref[pl.ds(n, 1), :]
        p_im = ti_ref[pl.ds(n, 1), :]
        s_ref[pl.ds(r0, nseq), 0:ns] += p_re * i_re - p_im * i_im
        s_ref[pl.ds(r0, nseq), ns:] += p_re * i_im + p_im * i_re
        return 0

    lax.fori_loop(0, nl, fix_step, 0)

    ya = yall_ref[...] + lax.dot_general(s_ref[...].astype(BF16), wct_ref[LANES:(lc + 1) * LANES, :],
                                         (((1,), (1,)), ((), ())), preferred_element_type=F32)
    ya = ya.reshape(nl, nseq, lc * LANES)
    yt = jnp.stack([ya[:, :, j * LANES:(j + 1) * LANES] for j in range(lc)], axis=1)
    y_ref[...] = jnp.swapaxes(yt.reshape(tseg, nseq, LANES), 0, 1).reshape(nseq * tseg, LANES)


def _s5(proj, prm, bsz, t):
    bt = bsz * t
    nsg = SUBLANES // bsz
    tseg = t // nsg
    nl = tseg // S5_CHUNK
    r = nl * SUBLANES
    lc = S5_CHUNK
    ns = S5_SLAB_STATE
    n_slab = prm["bb_re"].shape[0] // S5_SLAB_GROUPS
    u_blk0 = PROJ_U_COL // LANES
    grp = lambda shape: pl.BlockSpec((S5_SLAB_GROUPS,) + shape, lambda s: (s,) + (0,) * len(shape))
    vec = lambda rows: pl.BlockSpec((rows, ns), lambda s: (0, s))
    return pl.pallas_call(
        functools.partial(_s5_body, nl=nl, nsg=nsg, tseg=tseg),
        grid=(n_slab,),
        in_specs=[
            pl.BlockSpec((bt, LANES), lambda s: (0, u_blk0 + s)),
            pl.BlockSpec((lc + 1, S5_SLAB_GROUPS, LANES), lambda s: (0, s, 0)),
            pl.BlockSpec((lc + 1, S5_SLAB_GROUPS, LANES), lambda s: (0, s, 0)),
            grp((S5_GROUP_CH, LANES)), grp((S5_GROUP_CH, LANES)), grp((S5_GROUP_CH, LANES)), grp((S5_GROUP_CH, LANES)),
            vec(1), vec(1), vec(1), vec(1), vec(nl), vec(nl),
        ],
        out_specs=pl.BlockSpec((bt, LANES), lambda s: (0, s)),
        out_shape=jax.ShapeDtypeStruct((bt, n_slab * LANES), F32),
        scratch_shapes=[
            pltpu.VMEM((lc * LANES, 2 * ns), BF16),
            pltpu.VMEM(((lc + 1) * LANES, 2 * ns), BF16),
            pltpu.VMEM((LANES, 2 * ns), BF16),
            pltpu.VMEM((lc * LANES, lc * LANES), BF16),
            pltpu.VMEM((r, 2 * ns), F32),
            pltpu.VMEM((r, 2 * ns), F32),
            pltpu.VMEM((r, lc * LANES), F32),
        ],
        compiler_params=_cparams("arbitrary"),
        name="s5",
    )(proj, prm["pw_re"], prm["pw_im"], prm["bb_re"], prm["bb_im"], prm["c_re"], prm["c_im"],
      prm["ac_re"], prm["ac_im"], prm["as_re"], prm["as_im"], prm["pt_re"], prm["pt_im"])


def _s5_params(a_re, a_im, log_dt, b_re, b_im, c_re, c_im, nl):
    lc = S5_CHUNK
    nlay, ng, npst = a_re.shape
    dt = jnp.exp(log_dt)[..., None]
    lam_re, lam_im = a_re * dt, a_im * dt

    def apow(n):
        n = n.astype(F32)[None, :, None, None]
        mag = jnp.exp(lam_re[:, None] * n)
        return mag * jnp.cos(lam_im[:, None] * n), mag * jnp.sin(lam_im[:, None] * n)

    ab_re, ab_im = jnp.exp(lam_re) * jnp.cos(lam_im), jnp.exp(lam_re) * jnp.sin(lam_im)
    den = a_re * a_re + a_im * a_im
    f_re = ((ab_re - 1.0) * a_re + ab_im * a_im) / den
    f_im = (ab_im * a_re - (ab_re - 1.0) * a_im) / den
    bb_re = (f_re[..., None] * b_re - f_im[..., None] * b_im).transpose(0, 1, 3, 2)
    bb_im = (f_re[..., None] * b_im + f_im[..., None] * b_re).transpose(0, 1, 3, 2)
    odd = (jnp.arange(ng) % 2 == 1)[None, :, None, None]

    def half(x):
        z = jnp.zeros_like(x)
        return jnp.concatenate([jnp.where(odd, z, x), jnp.where(odd, x, z)], axis=-1)

    both = lambda x: jnp.concatenate([x, x], axis=-1)
    flat = lambda x: x.reshape(nlay, x.shape[1], ng * npst)
    pw_re, pw_im = apow(jnp.arange(lc + 1))
    ac_re, ac_im = apow(jnp.array([lc]))
    as_re, as_im = apow(jnp.array([lc * nl]))
    pt_re, pt_im = apow(jnp.arange(nl) * lc)
    return {
        "pw_re": both(pw_re), "pw_im": both(pw_im),
        "bb_re": half(bb_re), "bb_im": half(bb_im), "c_re": half(c_re), "c_im": half(c_im),
        "ac_re": flat(ac_re), "ac_im": flat(ac_im), "as_re": flat(as_re), "as_im": flat(as_im),
        "pt_re": flat(pt_re), "pt_im": flat(pt_im),
    }


def _outproj_body(m_ref, y_ref, u_ref, d_ref, wg_ref, bg_ref, wm_ref, ws_ref, x_ref, g_ref, xo_ref, ho_ref, *, ts):
    for r0 in range(0, m_ref.shape[0], ts):
        rows = slice(r0, r0 + ts)
        z = jax.nn.gelu(y_ref[rows, :] + d_ref[...] * u_ref[rows, :])
        gate = jnp.dot(z.astype(BF16), wg_ref[...], preferred_element_type=F32) + bg_ref[...]
        s_out = (z * jax.nn.sigmoid(gate)).astype(BF16)
        acc = jnp.dot(m_ref[rows, :], wm_ref[...], preferred_element_type=F32)
        acc = acc + jnp.dot(s_out, ws_ref[...], preferred_element_type=F32)
        x = x_ref[rows, :] + acc
        xo_ref[rows, :] = x
        ms = jnp.mean(x * x, axis=-1, keepdims=True)
        ho_ref[rows, :] = (x * lax.rsqrt(ms + EPS) * g_ref[...]).astype(ho_ref.dtype)


def _outproj(m_out, y, proj, s5_d, w_glu, b_glu, w, l, x, g, tm=512, ts=256):
    m, d = x.shape
    kh = m_out.shape[1]
    u_blk = PROJ_U_COL // kh
    const = lambda shape: pl.BlockSpec(shape, lambda i: (0,) * len(shape))
    return pl.pallas_call(
        functools.partial(_outproj_body, ts=ts),
        grid=(m // tm,),
        in_specs=[
            pl.BlockSpec((tm, kh), lambda i: (i, 0)),
            pl.BlockSpec((tm, kh), lambda i: (i, 0)),
            pl.BlockSpec((tm, kh), lambda i: (i, u_blk)),
            const((1, kh)),
            pl.BlockSpec((None, kh, kh), lambda i: (l, 0, 0)),
            const((1, kh)),
            pl.BlockSpec((None, kh, d), lambda i: (l, 0, 0)),
            pl.BlockSpec((None, kh, d), lambda i: (l, 1, 0)),
            pl.BlockSpec((tm, d), lambda i: (i, 0)),
            const((1, d)),
        ],
        out_specs=[pl.BlockSpec((tm, d), lambda i: (i, 0)), pl.BlockSpec((tm, d), lambda i: (i, 0))],
        out_shape=[jax.ShapeDtypeStruct((m, d), F32), jax.ShapeDtypeStruct((m, d), BF16)],
        compiler_params=_cparams("parallel"),
        name="out_proj",
    )(m_out, y, proj, s5_d, w_glu, b_glu, w, w, x, g)


def _ffn_up_body(h_ref, wg_ref, wv_ref, cw_ref, cb_ref, a_ref, wgb_ref, wvb_ref, ext_ref, *, tm, ts, tiles_per_seq):
    i = pl.program_id(1)

    @pl.when(i == 0)
    def _():
        wgb_ref[...] = wg_ref[...].astype(BF16)
        wvb_ref[...] = wv_ref[...].astype(BF16)

    @pl.when(i % tiles_per_seq == 0)
    def _():
        ext_ref[0:SUBLANES, :] = jnp.zeros((SUBLANES, ext_ref.shape[1]), F32)

    for r0 in range(0, tm, ts):
        hs = h_ref[r0:r0 + ts, :]
        g = jnp.dot(hs, wgb_ref[...], preferred_element_type=F32)
        ext_ref[SUBLANES + r0:SUBLANES + r0 + ts, :] = g
        conv = cb_ref[...] + cw_ref[F_CONV - 1:F_CONV, :] * g
        for kk in range(F_CONV - 1):
            conv = conv + cw_ref[kk:kk + 1, :] * ext_ref[pl.ds(SUBLANES + r0 - (F_CONV - 1) + kk, ts), :]
        val = jnp.dot(hs, wvb_ref[...], preferred_element_type=F32)
        a_ref[r0:r0 + ts, :] = (jax.nn.gelu(conv) * val).astype(a_ref.dtype)
    ext_ref[0:SUBLANES, :] = ext_ref[tm:tm + SUBLANES, :]


def _ffn_up(h, wg, wv, cw, cb, l, t, tm=2048, tn=512, ts=256):
    m, k = h.shape
    n = wg.shape[2]
    tm = min(tm, t)
    return pl.pallas_call(
        functools.partial(_ffn_up_body, tm=tm, ts=ts, tiles_per_seq=t // tm),
        grid=(pl.cdiv(n, tn), m // tm),
        in_specs=[
            pl.BlockSpec((tm, k), lambda j, i: (i, 0)),
            pl.BlockSpec((None, k, tn), lambda j, i: (l, 0, j)),
            pl.BlockSpec((None, k, tn), lambda j, i: (l, 0, j)),
            pl.BlockSpec((None, F_CONV, tn), lambda j, i: (l, 0, j)),
            pl.BlockSpec((None, 1, tn), lambda j, i: (l, 0, j)),
        ],
        out_specs=pl.BlockSpec((tm, tn), lambda j, i: (i, j)),
        out_shape=jax.ShapeDtypeStruct((m, n), BF16),
        scratch_shapes=[
            pltpu.VMEM((k, tn), BF16), pltpu.VMEM((k, tn), BF16),
            pltpu.VMEM((tm + SUBLANES, tn), F32),
        ],
        compiler_params=_cparams("parallel", "arbitrary"),
        name="ffn_up",
    )(h, wg, wv, cw, cb)


def _ffn_down_body(a_ref, w_ref, x_ref, xo_ref, wb_ref):
    @pl.when(pl.program_id(1) == 0)
    def _():
        wb_ref[...] = w_ref[...].astype(BF16)

    xo_ref[...] = x_ref[...] + jnp.dot(a_ref[...], wb_ref[...], preferred_element_type=F32)


def _ffn_down(a, w, l, x, tm=512, tn=512):
    m, d = x.shape
    k = a.shape[1]
    return pl.pallas_call(
        _ffn_down_body,
        grid=(d // tn, m // tm),
        in_specs=[
            pl.BlockSpec((tm, k), lambda j, i: (i, 0)),
            pl.BlockSpec((None, k, tn), lambda j, i: (l, 0, j)),
            pl.BlockSpec((tm, tn), lambda j, i: (i, j)),
        ],
        out_specs=pl.BlockSpec((tm, tn), lambda j, i: (i, j)),
        out_shape=jax.ShapeDtypeStruct((m, d), F32),
        scratch_shapes=[pltpu.VMEM((k, tn), BF16)],
        compiler_params=_cparams("parallel", "arbitrary"),
        name="ffn_down",
    )(a, w, x)


def _pad_cols(w, mult):
    pad = (-w.shape[-1]) % mult
    return jnp.pad(w, [(0, 0)] * (w.ndim - 1) + [(0, pad)])


def _trunk(x, norm_mix_g, w_in, m_conv_w, m_conv_b, w_q, w_k, m_i_bias, m_f_bias, m_head_g, m_skip,
           s5_a_re, s5_a_im, s5_log_dt, s5_b_re, s5_b_im, s5_c_re, s5_c_im, s5_d, s5_w_glu, s5_b_glu,
           w_out, norm_ffn_g, w_gate, w_val, f_conv_w, f_conv_b, w_down, norm_final_g):
    bsz, t, d = x.shape
    depth = w_in.shape[0]
    bt = bsz * t
    nl = t // (SUBLANES // bsz) // S5_CHUNK

    c3 = PROJ_U_COL
    w_tail = jnp.concatenate([
        w_in[:, :, c3 + 2 * M_HEADS:],
        _pad_cols(w_in[:, :, c3:c3 + M_HEADS], LANES), _pad_cols(w_in[:, :, c3 + M_HEADS:c3 + 2 * M_HEADS], LANES),
    ], axis=-1)
    w_tail = _pad_cols(w_tail, PROJ_COLS - c3).astype(BF16)
    w_head = w_in[:, :, :c3].astype(BF16)
    ib = _pad_cols(m_i_bias, LANES)[:, None, :]
    fb = _pad_cols(m_f_bias, LANES)[:, None, :]
    wq_b, wk_b = w_q.astype(BF16), w_k.astype(BF16)
    s5p = _s5_params(s5_a_re, s5_a_im, s5_log_dt, s5_b_re, s5_b_im, s5_c_re, s5_c_im, nl)
    wglu_b = s5_w_glu.astype(BF16)
    wout_b = w_out.astype(BF16)

    xf = x.reshape(bt, d)
    for l in range(depth):
        proj = _in_proj(xf, norm_mix_g[l][None], w_head, w_tail, l)
        m_out = _mlstm(proj, m_conv_w[l], m_conv_b[l][None], wq_b[l], wk_b[l], ib[l], fb[l],
                       m_head_g[l][None], m_skip[l][None], bsz, t)
        y = _s5(proj, {k_: v_[l] for k_, v_ in s5p.items()}, bsz, t)
        xf, h = _outproj(m_out, y, proj, s5_d[l][None], wglu_b, s5_b_glu[l][None], wout_b, l, xf, norm_ffn_g[l][None])
        a = _ffn_up(h, w_gate, w_val, f_conv_w, f_conv_b[:, None, :], l, t)
        xf = _ffn_down(a, w_down, l, xf)
    return _rmsnorm(xf, norm_final_g, F32).reshape(bsz, t, d)


def kernel(x, norm_mix_g, w_in, m_conv_w, m_conv_b, w_q, w_k, m_i_bias, m_f_bias, m_head_g, m_skip, s5_a_re, s5_a_im, s5_log_dt, s5_b_re, s5_b_im, s5_c_re, s5_c_im, s5_d, s5_w_glu, s5_b_glu, w_out, norm_ffn_g, w_gate, w_val, f_conv_w, f_conv_b, w_down, norm_final_g):
    return _trunk(x, norm_mix_g, w_in, m_conv_w, m_conv_b, w_q, w_k, m_i_bias, m_f_bias, m_head_g, m_skip,
                  s5_a_re, s5_a_im, s5_log_dt, s5_b_re, s5_b_im, s5_c_re, s5_c_im, s5_d, s5_w_glu, s5_b_glu,
                  w_out, norm_ffn_g, w_gate, w_val, f_conv_w, f_conv_b, w_down, norm_final_g)
```

```python
import functools
import math

import jax
import jax.numpy as jnp
from jax import lax
from jax.experimental import pallas as pl
from jax.experimental.pallas import tpu as pltpu

F32 = jnp.float32
BF16 = jnp.bfloat16
EPS = 1e-6

M_HEADS = 4
M_HEAD_V = 256
M_HEAD_QK = 128
M_WIDTH = M_HEADS * M_HEAD_V
M_CONV = 4
M_CHUNK = 128
S5_GROUP_CH = 16
S5_STATE = 64
S5_CH = 1024
S5_CHUNK = 8
F_CONV = 3

LANES = 128
SUBLANES = 8
PROJ_U_COL = 3 * M_WIDTH
PROJ_GATE_COL = PROJ_U_COL + S5_CH
PROJ_COLS = PROJ_GATE_COL + 2 * LANES
S5_SLAB_GROUPS = LANES // S5_GROUP_CH
S5_SLAB_STATE = S5_SLAB_GROUPS * S5_STATE
VMEM_LIMIT = 56 * 1024 * 1024


def _cparams(*sem):
    return pltpu.CompilerParams(dimension_semantics=sem, vmem_limit_bytes=VMEM_LIMIT)


def _rmsnorm_body(x_ref, g_ref, o_ref):
    x = x_ref[...]
    ms = jnp.mean(x * x, axis=-1, keepdims=True)
    o_ref[...] = (x * lax.rsqrt(ms + EPS) * g_ref[...]).astype(o_ref.dtype)


def _rmsnorm(x, g, out_dtype, tm=512):
    m, d = x.shape
    return pl.pallas_call(
        _rmsnorm_body,
        grid=(m // tm,),
        in_specs=[pl.BlockSpec((tm, d), lambda i: (i, 0)), pl.BlockSpec((1, d), lambda i: (0, 0))],
        out_specs=pl.BlockSpec((tm, d), lambda i: (i, 0)),
        out_shape=jax.ShapeDtypeStruct((m, d), out_dtype),
        compiler_params=_cparams("parallel"),
        name="rmsnorm",
    )(x, g.reshape(1, d))


def _in_proj_body(x_ref, g_ref, wh_ref, wt_ref, o_ref, *, ts):
    n_head = wh_ref.shape[1]
    for r0 in range(0, x_ref.shape[0], ts):
        x = x_ref[r0:r0 + ts, :]
        ms = jnp.mean(x * x, axis=-1, keepdims=True)
        h = (x * lax.rsqrt(ms + EPS) * g_ref[...]).astype(BF16)
        o_ref[r0:r0 + ts, :n_head] = jnp.dot(h, wh_ref[...], preferred_element_type=F32)
        o_ref[r0:r0 + ts, n_head:] = jnp.dot(h, wt_ref[...], preferred_element_type=F32)


def _in_proj(x, g, w_head, w_tail, l, tm=512, ts=256):
    m, k = x.shape
    n_head, n_tail = w_head.shape[2], w_tail.shape[2]
    resident = dict(pipeline_mode=pl.Buffered(1))
    return pl.pallas_call(
        functools.partial(_in_proj_body, ts=ts),
        grid=(m // tm,),
        in_specs=[
            pl.BlockSpec((tm, k), lambda i: (i, 0)),
            pl.BlockSpec((1, k), lambda i: (0, 0)),
            pl.BlockSpec((None, k, n_head), lambda i: (l, 0, 0), **resident),
            pl.BlockSpec((None, k, n_tail), lambda i: (l, 0, 0), **resident),
        ],
        out_specs=pl.BlockSpec((tm, n_head + n_tail), lambda i: (i, 0)),
        out_shape=jax.ShapeDtypeStruct((m, n_head + n_tail), F32),
        compiler_params=_cparams("parallel"),
        name="in_proj",
    )(x, g, w_head, w_tail)


def _log_sigmoid(x):
    return jnp.minimum(x, 0.0) - jnp.log1p(jnp.exp(-jnp.abs(x)))


def _mlstm_body(xm_ref, v_ref, o_ref, ig_ref, fg_ref, cw_ref, cb_ref, wq_ref, wk_ref, ib_ref, fb_ref,
                hg_ref, sk_ref, out_ref, ext_ref, c_ref, n_ref, m_ref, h_ref, *, tb):
    ncb = tb // M_CHUNK
    L = M_CHUNK

    @pl.when(pl.program_id(1) == 0)
    def _():
        ext_ref[0:SUBLANES, :] = jnp.zeros((SUBLANES, M_WIDTH), F32)
        c_ref[...] = jnp.zeros_like(c_ref)
        n_ref[...] = jnp.zeros_like(n_ref)
        m_ref[...] = jnp.zeros_like(m_ref)

    ext_ref[SUBLANES:, :] = xm_ref[...]
    acc = cb_ref[...] + cw_ref[0:1, :] * ext_ref[pl.ds(SUBLANES - 3, tb), :]
    for kk in range(1, M_CONV):
        acc = acc + cw_ref[kk:kk + 1, :] * ext_ref[pl.ds(SUBLANES - 3 + kk, tb), :]
    c = acc * jax.nn.sigmoid(acc)
    ext_ref[0:SUBLANES, :] = xm_ref[tb - SUBLANES:tb, :]

    i_pre = ig_ref[...] + ib_ref[...]
    lf = _log_sigmoid(fg_ref[...] + fb_ref[...])
    pos = lax.broadcasted_iota(jnp.int32, (tb, LANES), 0) % L
    bc = lf
    s = 1
    while s < L:
        bc = bc + jnp.where(pos >= s, pltpu.roll(bc, s, axis=0), 0.0)
        s *= 2
    rowv = i_pre - bc
    a = rowv
    s = 1
    while s < L:
        a = jnp.where(pos >= s, jnp.maximum(a, pltpu.roll(a, s, axis=0)), a)
        s *= 2
    bc3 = bc.reshape(ncb, L, LANES)
    g = bc3[:, L - 1:L, :]
    dec3 = g - bc3 + i_pre.reshape(ncb, L, LANES)
    maxdec = jnp.max(dec3, axis=1, keepdims=True)
    m_run = m_ref[0:1, :]
    m0_l, m1_l = [], []
    for nn in range(ncb):
        m0_l.append(m_run)
        m_run = jnp.maximum(g[nn] + m_run, maxdec[nn])
        m1_l.append(m_run)
    m_ref[0:1, :] = m_run
    m0 = jnp.stack(m0_l, axis=0)
    m1 = jnp.stack(m1_l, axis=0)
    mx = jnp.maximum(m0, a.reshape(ncb, L, LANES))
    colv = -mx
    w_inter = jnp.exp(m0 - mx)
    e_negm = jnp.exp(-(bc3 + mx))
    w_k = jnp.exp(dec3 - m1)
    w_c = jnp.exp(g + m0 - m1)
    rowv_t = rowv.T

    causal = lax.broadcasted_iota(jnp.int32, (L, L), 1) <= lax.broadcasted_iota(jnp.int32, (L, L), 0)
    scale = M_HEAD_QK ** -0.5

    for h in range(M_HEADS):
        c_h = c[:, h * M_HEAD_V:(h + 1) * M_HEAD_V].astype(BF16)
        q = jnp.dot(c_h, wq_ref[h], preferred_element_type=F32) * scale
        k = jnp.dot(c_h, wk_ref[h], preferred_element_type=F32)
        qb = q.astype(BF16)
        kb = k.astype(BF16)
        vb = v_ref[:, h * M_HEAD_V:(h + 1) * M_HEAD_V].astype(BF16)
        c_st = c_ref[h]
        n_st = n_ref[h]
        for nn in range(ncb):
            r0 = nn * L
            q_n = q[r0:r0 + L]
            qb_n = qb[r0:r0 + L]
            sc = lax.dot_general(qb_n, kb[r0:r0 + L], (((1,), (1,)), ((), ())),
                                 preferred_element_type=F32)
            arg = colv[nn][:, h:h + 1] + rowv_t[h:h + 1, r0:r0 + L]
            sw = sc * jnp.exp(jnp.where(causal, arg, -jnp.inf))
            wi = w_inter[nn][:, h:h + 1]
            num = jnp.dot(sw.astype(BF16), vb[r0:r0 + L], preferred_element_type=F32)
            num = num + wi * jnp.dot(qb_n, c_st.astype(BF16), preferred_element_type=F32)
            den = jnp.sum(sw, axis=1, keepdims=True) + wi * jnp.sum(q_n * n_st, axis=1, keepdims=True)
            den = jnp.maximum(jnp.abs(den), e_negm[nn][:, h:h + 1])
            h_ref[r0:r0 + L, h * M_HEAD_V:(h + 1) * M_HEAD_V] = num / den
            kw = k[r0:r0 + L] * w_k[nn][:, h:h + 1]
            wc = w_c[nn][:, h:h + 1]
            upd = lax.dot_general(kw.astype(BF16), vb[r0:r0 + L], (((0,), (0,)), ((), ())),
                                  preferred_element_type=F32)
            c_st = wc * c_st + upd
            n_st = wc * n_st + jnp.sum(kw, axis=0, keepdims=True)
        c_ref[h] = c_st
        n_ref[h] = n_st

    for h in range(M_HEADS):
        sl = slice(h * M_HEAD_V, (h + 1) * M_HEAD_V)
        hh = h_ref[:, sl]
        hn = hh * lax.rsqrt(jnp.mean(hh * hh, axis=-1, keepdims=True) + EPS) * hg_ref[:, sl]
        out = jax.nn.sigmoid(o_ref[:, sl]) * (hn + sk_ref[:, sl] * c[:, sl])
        out_ref[:, sl] = out.astype(out_ref.dtype)


def _mlstm(proj, cw, cb, wq, wk, ib, fb, hg, sk, bsz, t, tb=256):
    nt = t // tb
    gate_blk = PROJ_GATE_COL // LANES
    row = lambda b, i: b * nt + i
    full = lambda shape: pl.BlockSpec(shape, lambda b, i: (0,) * len(shape))
    return pl.pallas_call(
        functools.partial(_mlstm_body, tb=tb),
        grid=(bsz, nt),
        in_specs=[
            pl.BlockSpec((tb, M_WIDTH), lambda b, i: (row(b, i), 0)),
            pl.BlockSpec((tb, M_WIDTH), lambda b, i: (row(b, i), 1)),
            pl.BlockSpec((tb, M_WIDTH), lambda b, i: (row(b, i), 2)),
            pl.BlockSpec((tb, LANES), lambda b, i: (row(b, i), gate_blk)),
            pl.BlockSpec((tb, LANES), lambda b, i: (row(b, i), gate_blk + 1)),
            full((M_CONV, M_WIDTH)), full((1, M_WIDTH)),
            full((M_HEADS, M_HEAD_V, M_HEAD_QK)), full((M_HEADS, M_HEAD_V, M_HEAD_QK)),
            full((1, LANES)), full((1, LANES)), full((1, M_WIDTH)), full((1, M_WIDTH)),
        ],
        out_specs=pl.BlockSpec((tb, M_WIDTH), lambda b, i: (row(b, i), 0)),
        out_shape=jax.ShapeDtypeStruct((bsz * t, M_WIDTH), BF16),
        scratch_shapes=[
            pltpu.VMEM((tb + SUBLANES, M_WIDTH), F32),
            pltpu.VMEM((M_HEADS, M_HEAD_QK, M_HEAD_V), F32),
            pltpu.VMEM((M_HEADS, 1, M_HEAD_QK), F32),
            pltpu.VMEM((SUBLANES, LANES), F32),
            pltpu.VMEM((tb, M_WIDTH), F32),
        ],
        compiler_params=_cparams("parallel", "arbitrary"),
        name="mlstm",
    )(proj, proj, proj, proj, proj, cw, cb, wq, wk, ib, fb, hg, sk)


def _s5_body(u_ref, pre_ref, pim_ref, bre_ref, bim_ref, cre_ref, cim_ref, ar_ref, ai_ref, br_ref, bi_ref,
             tr_ref, ti_ref, y_ref, wst_ref, wct_ref, bbr_ref, toe_ref, v_ref, s_ref, yall_ref,
             *, nl, nsg, tseg):
    lc = S5_CHUNK
    ns = S5_SLAB_STATE
    gch = S5_GROUP_CH
    nseq = SUBLANES

    @pl.when(pl.program_id(0) == 0)
    def _():
        wst_ref[...] = jnp.zeros_like(wst_ref)
        wct_ref[...] = jnp.zeros_like(wct_ref)
        bbr_ref[...] = jnp.zeros_like(bbr_ref)
        toe_ref[...] = jnp.zeros_like(toe_ref)

    for gg in range(S5_SLAB_GROUPS):
        k0 = (gg // 2) * LANES
        bre, bim, cre, cim = bre_ref[gg], bim_ref[gg], cre_ref[gg], cim_ref[gg]
        r0 = gg * gch
        bbr_ref[r0:r0 + gch, k0:k0 + LANES] = bre.astype(BF16)
        bbr_ref[r0:r0 + gch, ns + k0:ns + k0 + LANES] = bim.astype(BF16)
        for j in range(lc + 1):
            p_re = pre_ref[j, gg:gg + 1, :]
            p_im = pim_ref[j, gg:gg + 1, :]
            r1 = j * LANES + r0
            wct_ref[r1:r1 + gch, k0:k0 + LANES] = (cre * p_re - cim * p_im).astype(BF16)
            wct_ref[r1:r1 + gch, ns + k0:ns + k0 + LANES] = (-(cre * p_im + cim * p_re)).astype(BF16)
            if j < lc:
                r2 = (lc - 1 - j) * LANES + r0
                wst_ref[r2:r2 + gch, k0:k0 + LANES] = (bre * p_re - bim * p_im).astype(BF16)
                wst_ref[r2:r2 + gch, ns + k0:ns + k0 + LANES] = (bre * p_im + bim * p_re).astype(BF16)
    kt = lax.dot_general(bbr_ref[...], wct_ref[0:lc * LANES, :], (((1,), (1,)), ((), ())),
                         preferred_element_type=F32).astype(BF16)
    for i in range(lc):
        toe_ref[i * LANES:(i + 1) * LANES, i * LANES:] = kt[:, :(lc - i) * LANES]

    ut = jnp.swapaxes(u_ref[...].reshape(nseq, tseg, LANES), 0, 1).reshape(nl, lc, nseq, LANES)
    uc = jnp.concatenate([ut[:, i] for i in range(lc)], axis=-1).reshape(nl * nseq, lc * LANES).astype(BF16)
    yall_ref[...] = jnp.dot(uc, toe_ref[...], preferred_element_type=F32)
    v_ref[...] = jnp.dot(uc, wst_ref[...], preferred_element_type=F32)

    ar = jnp.broadcast_to(ar_ref[...], (nseq, ns))
    ai = jnp.broadcast_to(ai_ref[...], (nseq, ns))

    def scan_step(n, carry):
        s_re, s_im = carry
        r0 = pl.multiple_of(n * nseq, nseq)
        s_ref[pl.ds(r0, nseq), 0:ns] = s_re
        s_ref[pl.ds(r0, nseq), ns:] = s_im
        n_re = ar * s_re - ai * s_im + v_ref[pl.ds(r0, nseq), 0:ns]
        n_im = ar * s_im + ai * s_re + v_ref[pl.ds(r0, nseq), ns:]
        return n_re, n_im

    zero = jnp.zeros((nseq, ns), F32)
    f_re, f_im = lax.fori_loop(0, nl, scan_step, (zero, zero))

    br = jnp.broadcast_to(br_ref[...], (nseq, ns))
    bi = jnp.broadcast_to(bi_ref[...], (nseq, ns))
    has_prev = lax.broadcasted_iota(jnp.int32, (nseq, ns), 0) % nsg >= 1
    i_re, i_im = zero, zero
    for _ in range(nsg - 1):
        t_re = f_re + br * i_re - bi * i_im
        t_im = f_im + br * i_im + bi * i_re
        i_re = jnp.where(has_prev, pltpu.roll(t_re, 1, axis=0), 0.0)
        i_im = jnp.where(has_prev, pltpu.roll(t_im, 1, axis=0), 0.0)

    def fix_step(n, _):
        r0 = pl.multiple_of(n * nseq, nseq)
        p_re = tr_ref[pl.ds(n, 1), :]
        p_im = ti_ref[pl.ds(n, 1), :]
        s_ref[pl.ds(r0, nseq), 0:ns] += p_re * i_re - p_im * i_im
        s_ref[pl.ds(r0, nseq), ns:] += p_re * i_im + p_im * i_re
        return 0

    lax.fori_loop(0, nl, fix_step, 0)

    ya = yall_ref[...] + lax.dot_general(s_ref[...].astype(BF16), wct_ref[LANES:(lc + 1) * LANES, :],
                                         (((1,), (1,)), ((), ())), preferred_element_type=F32)
    ya = ya.reshape(nl, nseq, lc * LANES)
    yt = jnp.stack([ya[:, :, j * LANES:(j + 1) * LANES] for j in range(lc)], axis=1)
    y_ref[...] = jnp.swapaxes(yt.reshape(tseg, nseq, LANES), 0, 1).reshape(nseq * tseg, LANES)


def _s5(proj, prm, bsz, t):
    bt = bsz * t
    nsg = SUBLANES // bsz
    tseg = t // nsg
    nl = tseg // S5_CHUNK
    r = nl * SUBLANES
    lc = S5_CHUNK
    ns = S5_SLAB_STATE
    n_slab = prm["bb_re"].shape[0] // S5_SLAB_GROUPS
    u_blk0 = PROJ_U_COL // LANES
    grp = lambda shape: pl.BlockSpec((S5_SLAB_GROUPS,) + shape, lambda s: (s,) + (0,) * len(shape))
    vec = lambda rows: pl.BlockSpec((rows, ns), lambda s: (0, s))
    return pl.pallas_call(
        functools.partial(_s5_body, nl=nl, nsg=nsg, tseg=tseg),
        grid=(n_slab,),
        in_specs=[
            pl.BlockSpec((bt, LANES), lambda s: (0, u_blk0 + s)),
            pl.BlockSpec((lc + 1, S5_SLAB_GROUPS, LANES), lambda s: (0, s, 0)),
            pl.BlockSpec((lc + 1, S5_SLAB_GROUPS, LANES), lambda s: (0, s, 0)),
            grp((S5_GROUP_CH, LANES)), grp((S5_GROUP_CH, LANES)), grp((S5_GROUP_CH, LANES)), grp((S5_GROUP_CH, LANES)),
            vec(1), vec(1), vec(1), vec(1), vec(nl), vec(nl),
        ],
        out_specs=pl.BlockSpec((bt, LANES), lambda s: (0, s)),
        out_shape=jax.ShapeDtypeStruct((bt, n_slab * LANES), F32),
        scratch_shapes=[
            pltpu.VMEM((lc * LANES, 2 * ns), BF16),
            pltpu.VMEM(((lc + 1) * LANES, 2 * ns), BF16),
            pltpu.VMEM((LANES, 2 * ns), BF16),
            pltpu.VMEM((lc * LANES, lc * LANES), BF16),
            pltpu.VMEM((r, 2 * ns), F32),
            pltpu.VMEM((r, 2 * ns), F32),
            pltpu.VMEM((r, lc * LANES), F32),
        ],
        compiler_params=_cparams("arbitrary"),
        name="s5",
    )(proj, prm["pw_re"], prm["pw_im"], prm["bb_re"], prm["bb_im"], prm["c_re"], prm["c_im"],
      prm["ac_re"], prm["ac_im"], prm["as_re"], prm["as_im"], prm["pt_re"], prm["pt_im"])


def _s5_params(a_re, a_im, log_dt, b_re, b_im, c_re, c_im, nl):
    lc = S5_CHUNK
    nlay, ng, npst = a_re.shape
    dt = jnp.exp(log_dt)[..., None]
    lam_re, lam_im = a_re * dt, a_im * dt

    def apow(n):
        n = n.astype(F32)[None, :, None, None]
        mag = jnp.exp(lam_re[:, None] * n)
        return mag * jnp.cos(lam_im[:, None] * n), mag * jnp.sin(lam_im[:, None] * n)

    ab_re, ab_im = jnp.exp(lam_re) * jnp.cos(lam_im), jnp.exp(lam_re) * jnp.sin(lam_im)
    den = a_re * a_re + a_im * a_im
    f_re = ((ab_re - 1.0) * a_re + ab_im * a_im) / den
    f_im = (ab_im * a_re - (ab_re - 1.0) * a_im) / den
    bb_re = (f_re[..., None] * b_re - f_im[..., None] * b_im).transpose(0, 1, 3, 2)
    bb_im = (f_re[..., None] * b_im + f_im[..., None] * b_re).transpose(0, 1, 3, 2)
    odd = (jnp.arange(ng) % 2 == 1)[None, :, None, None]

    def half(x):
        z = jnp.zeros_like(x)
        return jnp.concatenate([jnp.where(odd, z, x), jnp.where(odd, x, z)], axis=-1)

    both = lambda x: jnp.concatenate([x, x], axis=-1)
    flat = lambda x: x.reshape(nlay, x.shape[1], ng * npst)
    pw_re, pw_im = apow(jnp.arange(lc + 1))
    ac_re, ac_im = apow(jnp.array([lc]))
    as_re, as_im = apow(jnp.array([lc * nl]))
    pt_re, pt_im = apow(jnp.arange(nl) * lc)
    return {
        "pw_re": both(pw_re), "pw_im": both(pw_im),
        "bb_re": half(bb_re), "bb_im": half(bb_im), "c_re": half(c_re), "c_im": half(c_im),
        "ac_re": flat(ac_re), "ac_im": flat(ac_im), "as_re": flat(as_re), "as_im": flat(as_im),
        "pt_re": flat(pt_re), "pt_im": flat(pt_im),
    }


def _outproj_body(m_ref, y_ref, u_ref, d_ref, wg_ref, bg_ref, wm_ref, ws_ref, x_ref, g_ref, xo_ref, ho_ref, *, ts):
    for r0 in range(0, m_ref.shape[0], ts):
        rows = slice(r0, r0 + ts)
        z = jax.nn.gelu(y_ref[rows, :] + d_ref[...] * u_ref[rows, :])
        gate = jnp.dot(z.astype(BF16), wg_ref[...], preferred_element_type=F32) + bg_ref[...]
        s_out = (z * jax.nn.sigmoid(gate)).astype(BF16)
        acc = jnp.dot(m_ref[rows, :], wm_ref[...], preferred_element_type=F32)
        acc = acc + jnp.dot(s_out, ws_ref[...], preferred_element_type=F32)
        x = x_ref[rows, :] + acc
        xo_ref[rows, :] = x
        ms = jnp.mean(x * x, axis=-1, keepdims=True)
        ho_ref[rows, :] = (x * lax.rsqrt(ms + EPS) * g_ref[...]).astype(ho_ref.dtype)


def _outproj(m_out, y, proj, s5_d, w_glu, b_glu, w, l, x, g, tm=512, ts=256):
    m, d = x.shape
    kh = m_out.shape[1]
    u_blk = PROJ_U_COL // kh
    const = lambda shape: pl.BlockSpec(shape, lambda i: (0,) * len(shape))
    return pl.pallas_call(
        functools.partial(_outproj_body, ts=ts),
        grid=(m // tm,),
        in_specs=[
            pl.BlockSpec((tm, kh), lambda i: (i, 0)),
            pl.BlockSpec((tm, kh), lambda i: (i, 0)),
            pl.BlockSpec((tm, kh), lambda i: (i, u_blk)),
            const((1, kh)),
            pl.BlockSpec((None, kh, kh), lambda i: (l, 0, 0)),
            const((1, kh)),
            pl.BlockSpec((None, kh, d), lambda i: (l, 0, 0)),
            pl.BlockSpec((None, kh, d), lambda i: (l, 1, 0)),
            pl.BlockSpec((tm, d), lambda i: (i, 0)),
            const((1, d)),
        ],
        out_specs=[pl.BlockSpec((tm, d), lambda i: (i, 0)), pl.BlockSpec((tm, d), lambda i: (i, 0))],
        out_shape=[jax.ShapeDtypeStruct((m, d), F32), jax.ShapeDtypeStruct((m, d), BF16)],
        compiler_params=_cparams("parallel"),
        name="out_proj",
    )(m_out, y, proj, s5_d, w_glu, b_glu, w, w, x, g)


def _ffn_up_body(h_ref, wg_ref, wv_ref, cw_ref, cb_ref, a_ref, wgb_ref, wvb_ref, ext_ref, *, tm, ts, tiles_per_seq):
    i = pl.program_id(1)

    @pl.when(i == 0)
    def _():
        wgb_ref[...] = wg_ref[...].astype(BF16)
        wvb_ref[...] = wv_ref[...].astype(BF16)

    @pl.when(i % tiles_per_seq == 0)
    def _():
        ext_ref[0:SUBLANES, :] = jnp.zeros((SUBLANES, ext_ref.shape[1]), F32)

    for r0 in range(0, tm, ts):
        hs = h_ref[r0:r0 + ts, :]
        g = jnp.dot(hs, wgb_ref[...], preferred_element_type=F32)
        ext_ref[SUBLANES + r0:SUBLANES + r0 + ts, :] = g
        conv = cb_ref[...] + cw_ref[F_CONV - 1:F_CONV, :] * g
        for kk in range(F_CONV - 1):
            conv = conv + cw_ref[kk:kk + 1, :] * ext_ref[pl.ds(SUBLANES + r0 - (F_CONV - 1) + kk, ts), :]
        val = jnp.dot(hs, wvb_ref[...], preferred_element_type=F32)
        a_ref[r0:r0 + ts, :] = (jax.nn.gelu(conv) * val).astype(a_ref.dtype)
    ext_ref[0:SUBLANES, :] = ext_ref[tm:tm + SUBLANES, :]


def _ffn_up(h, wg, wv, cw, cb, l, t, tm=2048, tn=512, ts=256):
    m, k = h.shape
    n = wg.shape[2]
    tm = min(tm, t)
    return pl.pallas_call(
        functools.partial(_ffn_up_body, tm=tm, ts=ts, tiles_per_seq=t // tm),
        grid=(pl.cdiv(n, tn), m // tm),
        in_specs=[
            pl.BlockSpec((tm, k), lambda j, i: (i, 0)),
            pl.BlockSpec((None, k, tn), lambda j, i: (l, 0, j)),
            pl.BlockSpec((None, k, tn), lambda j, i: (l, 0, j)),
            pl.BlockSpec((None, F_CONV, tn), lambda j, i: (l, 0, j)),
            pl.BlockSpec((None, 1, tn), lambda j, i: (l, 0, j)),
        ],
        out_specs=pl.BlockSpec((tm, tn), lambda j, i: (i, j)),
        out_shape=jax.ShapeDtypeStruct((m, n), BF16),
        scratch_shapes=[
            pltpu.VMEM((k, tn), BF16), pltpu.VMEM((k, tn), BF16),
            pltpu.VMEM((tm + SUBLANES, tn), F32),
        ],
        compiler_params=_cparams("parallel", "arbitrary"),
        name="ffn_up",
    )(h, wg, wv, cw, cb)


def _ffn_down_body(a_ref, w_ref, x_ref, xo_ref, wb_ref):
    @pl.when(pl.program_id(1) == 0)
    def _():
        wb_ref[...] = w_ref[...].astype(BF16)

    xo_ref[...] = x_ref[...] + jnp.dot(a_ref[...], wb_ref[...], preferred_element_type=F32)


def _ffn_down(a, w, l, x, tm=512, tn=512):
    m, d = x.shape
    k = a.shape[1]
    return pl.pallas_call(
        _ffn_down_body,
        grid=(d // tn, m // tm),
        in_specs=[
            pl.BlockSpec((tm, k), lambda j, i: (i, 0)),
            pl.BlockSpec((None, k, tn), lambda j, i: (l, 0, j)),
            pl.BlockSpec((tm, tn), lambda j, i: (i, j)),
        ],
        out_specs=pl.BlockSpec((tm, tn), lambda j, i: (i, j)),
        out_shape=jax.ShapeDtypeStruct((m, d), F32),
        scratch_shapes=[pltpu.VMEM((k, tn), BF16)],
        compiler_params=_cparams("parallel", "arbitrary"),
        name="ffn_down",
    )(a, w, x)


def _pad_cols(w, mult):
    pad = (-w.shape[-1]) % mult
    return jnp.pad(w, [(0, 0)] * (w.ndim - 1) + [(0, pad)])


def _trunk(x, norm_mix_g, w_in, m_conv_w, m_conv_b, w_q, w_k, m_i_bias, m_f_bias, m_head_g, m_skip,
           s5_a_re, s5_a_im, s5_log_dt, s5_b_re, s5_b_im, s5_c_re, s5_c_im, s5_d, s5_w_glu, s5_b_glu,
           w_out, norm_ffn_g, w_gate, w_val, f_conv_w, f_conv_b, w_down, norm_final_g):
    bsz, t, d = x.shape
    depth = w_in.shape[0]
    bt = bsz * t
    nl = t // (SUBLANES // bsz) // S5_CHUNK

    c3 = PROJ_U_COL
    w_tail = jnp.concatenate([
        w_in[:, :, c3 + 2 * M_HEADS:],
        _pad_cols(w_in[:, :, c3:c3 + M_HEADS], LANES), _pad_cols(w_in[:, :, c3 + M_HEADS:c3 + 2 * M_HEADS], LANES),
    ], axis=-1).astype(BF16)
    w_head = w_in[:, :, :c3].astype(BF16)
    ib = _pad_cols(m_i_bias, LANES)[:, None, :]
    fb = _pad_cols(m_f_bias, LANES)[:, None, :]
    wq_b, wk_b = w_q.astype(BF16), w_k.astype(BF16)
    s5p = _s5_params(s5_a_re, s5_a_im, s5_log_dt, s5_b_re, s5_b_im, s5_c_re, s5_c_im, nl)
    wglu_b = s5_w_glu.astype(BF16)
    wout_b = w_out.astype(BF16)

    xf = x.reshape(bt, d)
    for l in range(depth):
        proj = _in_proj(xf, norm_mix_g[l][None], w_head, w_tail, l)
        m_out = _mlstm(proj, m_conv_w[l], m_conv_b[l][None], wq_b[l], wk_b[l], ib[l], fb[l],
                       m_head_g[l][None], m_skip[l][None], bsz, t)
        y = _s5(proj, {k_: v_[l] for k_, v_ in s5p.items()}, bsz, t)
        xf, h = _outproj(m_out, y, proj, s5_d[l][None], wglu_b, s5_b_glu[l][None], wout_b, l, xf, norm_ffn_g[l][None])
        a = _ffn_up(h, w_gate, w_val, f_conv_w, f_conv_b[:, None, :], l, t)
        xf = _ffn_down(a, w_down, l, xf)
    return _rmsnorm(xf, norm_final_g, F32).reshape(bsz, t, d)


def kernel(x, norm_mix_g, w_in, m_conv_w, m_conv_b, w_q, w_k, m_i_bias, m_f_bias, m_head_g, m_skip, s5_a_re, s5_a_im, s5_log_dt, s5_b_re, s5_b_im, s5_c_re, s5_c_im, s5_d, s5_w_glu, s5_b_glu, w_out, norm_ffn_g, w_gate, w_val, f_conv_w, f_conv_b, w_down, norm_final_g):
    return _trunk(x, norm_mix_g, w_in, m_conv_w, m_conv_b, w_q, w_k, m_i_bias, m_f_bias, m_head_g, m_skip,
                  s5_a_re, s5_a_im, s5_log_dt, s5_b_re, s5_b_im, s5_c_re, s5_c_im, s5_d, s5_w_glu, s5_b_glu,
                  w_out, norm_ffn_g, w_gate, w_val, f_conv_w, f_conv_b, w_down, norm_final_g)
```

```python
import functools
import math

import jax
import jax.numpy as jnp
from jax import lax
from jax.experimental import pallas as pl
from jax.experimental.pallas import tpu as pltpu

F32 = jnp.float32
BF16 = jnp.bfloat16
EPS = 1e-6

M_HEADS = 4
M_HEAD_V = 256
M_HEAD_QK = 128
M_WIDTH = M_HEADS * M_HEAD_V
M_CONV = 4
M_CHUNK = 128
S5_GROUP_CH = 16
S5_STATE = 64
S5_CH = 1024
S5_CHUNK = 8
F_CONV = 3

LANES = 128
SUBLANES = 8
PROJ_U_COL = 3 * M_WIDTH
PROJ_GATE_COL = PROJ_U_COL + S5_CH
PROJ_COLS = PROJ_GATE_COL + 2 * LANES
S5_SLAB_GROUPS = LANES // S5_GROUP_CH
S5_SLAB_STATE = S5_SLAB_GROUPS * S5_STATE
VMEM_LIMIT = 56 * 1024 * 1024


def _cparams(*sem):
    return pltpu.CompilerParams(dimension_semantics=sem, vmem_limit_bytes=VMEM_LIMIT)


def _rmsnorm_body(x_ref, g_ref, o_ref):
    x = x_ref[...]
    ms = jnp.mean(x * x, axis=-1, keepdims=True)
    o_ref[...] = (x * lax.rsqrt(ms + EPS) * g_ref[...]).astype(o_ref.dtype)


def _rmsnorm(x, g, out_dtype, tm=512):
    m, d = x.shape
    return pl.pallas_call(
        _rmsnorm_body,
        grid=(m // tm,),
        in_specs=[pl.BlockSpec((tm, d), lambda i: (i, 0)), pl.BlockSpec((1, d), lambda i: (0, 0))],
        out_specs=pl.BlockSpec((tm, d), lambda i: (i, 0)),
        out_shape=jax.ShapeDtypeStruct((m, d), out_dtype),
        compiler_params=_cparams("parallel"),
        name="rmsnorm",
    )(x, g.reshape(1, d))


def _in_proj_body(x_ref, g_ref, wh_ref, wt_ref, o_ref, *, ts):
    n_head = wh_ref.shape[1]
    for r0 in range(0, x_ref.shape[0], ts):
        x = x_ref[r0:r0 + ts, :]
        ms = jnp.mean(x * x, axis=-1, keepdims=True)
        h = (x * lax.rsqrt(ms + EPS) * g_ref[...]).astype(BF16)
        o_ref[r0:r0 + ts, :n_head] = jnp.dot(h, wh_ref[...], preferred_element_type=F32)
        o_ref[r0:r0 + ts, n_head:] = jnp.dot(h, wt_ref[...], preferred_element_type=F32)


def _in_proj(x, g, w_head, w_tail, l, tm=512, ts=256):
    m, k = x.shape
    n_head, n_tail = PROJ_U_COL, w_tail.shape[2]
    resident = dict(pipeline_mode=pl.Buffered(1))
    return pl.pallas_call(
        functools.partial(_in_proj_body, ts=ts),
        grid=(m // tm,),
        in_specs=[
            pl.BlockSpec((tm, k), lambda i: (i, 0)),
            pl.BlockSpec((1, k), lambda i: (0, 0)),
            pl.BlockSpec((None, k, n_head), lambda i: (l, 0, 0), **resident),
            pl.BlockSpec((None, k, n_tail), lambda i: (l, 0, 0), **resident),
        ],
        out_specs=pl.BlockSpec((tm, n_head + n_tail), lambda i: (i, 0)),
        out_shape=jax.ShapeDtypeStruct((m, n_head + n_tail), F32),
        compiler_params=_cparams("parallel"),
        name="in_proj",
    )(x, g, w_head, w_tail)


def _log_sigmoid(x):
    return jnp.minimum(x, 0.0) - jnp.log1p(jnp.exp(-jnp.abs(x)))


def _mlstm_body(xm_ref, v_ref, o_ref, ig_ref, fg_ref, cw_ref, cb_ref, wq_ref, wk_ref, ib_ref, fb_ref,
                hg_ref, sk_ref, out_ref, ext_ref, c_ref, n_ref, m_ref, h_ref, *, tb):
    ncb = tb // M_CHUNK
    L = M_CHUNK

    @pl.when(pl.program_id(1) == 0)
    def _():
        ext_ref[0:SUBLANES, :] = jnp.zeros((SUBLANES, M_WIDTH), F32)
        c_ref[...] = jnp.zeros_like(c_ref)
        n_ref[...] = jnp.zeros_like(n_ref)
        m_ref[...] = jnp.zeros_like(m_ref)

    ext_ref[SUBLANES:, :] = xm_ref[...]
    acc = cb_ref[...] + cw_ref[0:1, :] * ext_ref[pl.ds(SUBLANES - 3, tb), :]
    for kk in range(1, M_CONV):
        acc = acc + cw_ref[kk:kk + 1, :] * ext_ref[pl.ds(SUBLANES - 3 + kk, tb), :]
    c = acc * jax.nn.sigmoid(acc)
    ext_ref[0:SUBLANES, :] = xm_ref[tb - SUBLANES:tb, :]

    i_pre = ig_ref[...] + ib_ref[...]
    lf = _log_sigmoid(fg_ref[...] + fb_ref[...])
    pos = lax.broadcasted_iota(jnp.int32, (tb, LANES), 0) % L
    bc = lf
    s = 1
    while s < L:
        bc = bc + jnp.where(pos >= s, pltpu.roll(bc, s, axis=0), 0.0)
        s *= 2
    rowv = i_pre - bc
    a = rowv
    s = 1
    while s < L:
        a = jnp.where(pos >= s, jnp.maximum(a, pltpu.roll(a, s, axis=0)), a)
        s *= 2
    bc3 = bc.reshape(ncb, L, LANES)
    g = bc3[:, L - 1:L, :]
    dec3 = g - bc3 + i_pre.reshape(ncb, L, LANES)
    maxdec = jnp.max(dec3, axis=1, keepdims=True)
    m_run = m_ref[0:1, :]
    m0_l, m1_l = [], []
    for nn in range(ncb):
        m0_l.append(m_run)
        m_run = jnp.maximum(g[nn] + m_run, maxdec[nn])
        m1_l.append(m_run)
    m_ref[0:1, :] = m_run
    m0 = jnp.stack(m0_l, axis=0)
    m1 = jnp.stack(m1_l, axis=0)
    mx = jnp.maximum(m0, a.reshape(ncb, L, LANES))
    colv = -mx
    w_inter = jnp.exp(m0 - mx)
    e_negm = jnp.exp(-(bc3 + mx))
    w_k = jnp.exp(dec3 - m1)
    w_c = jnp.exp(g + m0 - m1)
    rowv_t = rowv.T

    causal = lax.broadcasted_iota(jnp.int32, (L, L), 1) <= lax.broadcasted_iota(jnp.int32, (L, L), 0)
    scale = M_HEAD_QK ** -0.5

    for h in range(M_HEADS):
        c_h = c[:, h * M_HEAD_V:(h + 1) * M_HEAD_V].astype(BF16)
        q = jnp.dot(c_h, wq_ref[h], preferred_element_type=F32) * scale
        k = jnp.dot(c_h, wk_ref[h], preferred_element_type=F32)
        qb = q.astype(BF16)
        kb = k.astype(BF16)
        vb = v_ref[:, h * M_HEAD_V:(h + 1) * M_HEAD_V].astype(BF16)
        c_st = c_ref[h]
        n_st = n_ref[h]
        for nn in range(ncb):
            r0 = nn * L
            q_n = q[r0:r0 + L]
            qb_n = qb[r0:r0 + L]
            sc = lax.dot_general(qb_n, kb[r0:r0 + L], (((1,), (1,)), ((), ())),
                                 preferred_element_type=F32)
            arg = colv[nn][:, h:h + 1] + rowv_t[h:h + 1, r0:r0 + L]
            sw = sc * jnp.exp(jnp.where(causal, arg, -jnp.inf))
            wi = w_inter[nn][:, h:h + 1]
            num = jnp.dot(sw.astype(BF16), vb[r0:r0 + L], preferred_element_type=F32)
            num = num + wi * jnp.dot(qb_n, c_st.astype(BF16), preferred_element_type=F32)
            den = jnp.sum(sw, axis=1, keepdims=True) + wi * jnp.sum(q_n * n_st, axis=1, keepdims=True)
            den = jnp.maximum(jnp.abs(den), e_negm[nn][:, h:h + 1])
            h_ref[r0:r0 + L, h * M_HEAD_V:(h + 1) * M_HEAD_V] = num / den
            kw = k[r0:r0 + L] * w_k[nn][:, h:h + 1]
            wc = w_c[nn][:, h:h + 1]
            upd = lax.dot_general(kw.astype(BF16), vb[r0:r0 + L], (((0,), (0,)), ((), ())),
                                  preferred_element_type=F32)
            c_st = wc * c_st + upd
            n_st = wc * n_st + jnp.sum(kw, axis=0, keepdims=True)
        c_ref[h] = c_st
        n_ref[h] = n_st

    for h in range(M_HEADS):
        sl = slice(h * M_HEAD_V, (h + 1) * M_HEAD_V)
        hh = h_ref[:, sl]
        hn = hh * lax.rsqrt(jnp.mean(hh * hh, axis=-1, keepdims=True) + EPS) * hg_ref[:, sl]
        out = jax.nn.sigmoid(o_ref[:, sl]) * (hn + sk_ref[:, sl] * c[:, sl])
        out_ref[:, sl] = out.astype(out_ref.dtype)


def _mlstm(proj, cw, cb, wq, wk, ib, fb, hg, sk, bsz, t, tb=256):
    nt = t // tb
    gate_blk = PROJ_GATE_COL // LANES
    row = lambda b, i: b * nt + i
    full = lambda shape: pl.BlockSpec(shape, lambda b, i: (0,) * len(shape))
    return pl.pallas_call(
        functools.partial(_mlstm_body, tb=tb),
        grid=(bsz, nt),
        in_specs=[
            pl.BlockSpec((tb, M_WIDTH), lambda b, i: (row(b, i), 0)),
            pl.BlockSpec((tb, M_WIDTH), lambda b, i: (row(b, i), 1)),
            pl.BlockSpec((tb, M_WIDTH), lambda b, i: (row(b, i), 2)),
            pl.BlockSpec((tb, LANES), lambda b, i: (row(b, i), gate_blk)),
            pl.BlockSpec((tb, LANES), lambda b, i: (row(b, i), gate_blk + 1)),
            full((M_CONV, M_WIDTH)), full((1, M_WIDTH)),
            full((M_HEADS, M_HEAD_V, M_HEAD_QK)), full((M_HEADS, M_HEAD_V, M_HEAD_QK)),
            full((1, LANES)), full((1, LANES)), full((1, M_WIDTH)), full((1, M_WIDTH)),
        ],
        out_specs=pl.BlockSpec((tb, M_WIDTH), lambda b, i: (row(b, i), 0)),
        out_shape=jax.ShapeDtypeStruct((bsz * t, M_WIDTH), BF16),
        scratch_shapes=[
            pltpu.VMEM((tb + SUBLANES, M_WIDTH), F32),
            pltpu.VMEM((M_HEADS, M_HEAD_QK, M_HEAD_V), F32),
            pltpu.VMEM((M_HEADS, 1, M_HEAD_QK), F32),
            pltpu.VMEM((SUBLANES, LANES), F32),
            pltpu.VMEM((tb, M_WIDTH), F32),
        ],
        compiler_params=_cparams("parallel", "arbitrary"),
        name="mlstm",
    )(proj, proj, proj, proj, proj, cw, cb, wq, wk, ib, fb, hg, sk)


def _s5_body(u_ref, pre_ref, pim_ref, bre_ref, bim_ref, cre_ref, cim_ref, ar_ref, ai_ref, br_ref, bi_ref,
             y_ref, wst_ref, wct_ref, bbr_ref, toe_ref, v_ref, s_ref, yall_ref,
             *, nl, nsg, tseg):
    lc = S5_CHUNK
    ns = S5_SLAB_STATE
    gch = S5_GROUP_CH
    nseq = SUBLANES

    @pl.when(pl.program_id(0) == 0)
    def _():
        wst_ref[...] = jnp.zeros_like(wst_ref)
        wct_ref[...] = jnp.zeros_like(wct_ref)
        bbr_ref[...] = jnp.zeros_like(bbr_ref)
        toe_ref[...] = jnp.zeros_like(toe_ref)

    for gg in range(S5_SLAB_GROUPS):
        k0 = (gg // 2) * LANES
        bre, bim, cre, cim = bre_ref[gg], bim_ref[gg], cre_ref[gg], cim_ref[gg]
        r0 = gg * gch
        bbr_ref[r0:r0 + gch, k0:k0 + LANES] = bre.astype(BF16)
        bbr_ref[r0:r0 + gch, ns + k0:ns + k0 + LANES] = bim.astype(BF16)
        for j in range(lc + 1):
            p_re = pre_ref[j, gg:gg + 1, :]
            p_im = pim_ref[j, gg:gg + 1, :]
            r1 = j * LANES + r0
            wct_ref[r1:r1 + gch, k0:k0 + LANES] = (cre * p_re - cim * p_im).astype(BF16)
            wct_ref[r1:r1 + gch, ns + k0:ns + k0 + LANES] = (-(cre * p_im + cim * p_re)).astype(BF16)
            if j < lc:
                r2 = (lc - 1 - j) * LANES + r0
                wst_ref[r2:r2 + gch, k0:k0 + LANES] = (bre * p_re - bim * p_im).astype(BF16)
                wst_ref[r2:r2 + gch, ns + k0:ns + k0 + LANES] = (bre * p_im + bim * p_re).astype(BF16)
    kt = lax.dot_general(bbr_ref[...], wct_ref[0:lc * LANES, :], (((1,), (1,)), ((), ())),
                         preferred_element_type=F32).astype(BF16)
    for i in range(lc):
        toe_ref[i * LANES:(i + 1) * LANES, i * LANES:] = kt[:, :(lc - i) * LANES]

    ut = jnp.swapaxes(u_ref[...].reshape(nseq, tseg, LANES), 0, 1).reshape(nl, lc, nseq, LANES)
    uc = jnp.concatenate([ut[:, i] for i in range(lc)], axis=-1).reshape(nl * nseq, lc * LANES).astype(BF16)
    yall_ref[...] = jnp.dot(uc, toe_ref[...], preferred_element_type=F32)
    v_ref[...] = jnp.dot(uc, wst_ref[...], preferred_element_type=F32)

    ar = jnp.broadcast_to(ar_ref[...], (nseq, ns))
    ai = jnp.broadcast_to(ai_ref[...], (nseq, ns))

    def scan_step(n, carry):
        s_re, s_im = carry
        r0 = pl.multiple_of(n * nseq, nseq)
        s_ref[pl.ds(r0, nseq), 0:ns] = s_re
        s_ref[pl.ds(r0, nseq), ns:] = s_im
        n_re = ar * s_re - ai * s_im + v_ref[pl.ds(r0, nseq), 0:ns]
        n_im = ar * s_im + ai * s_re + v_ref[pl.ds(r0, nseq), ns:]
        return n_re, n_im

    zero = jnp.zeros((nseq, ns), F32)
    f_re, f_im = lax.fori_loop(0, nl, scan_step, (zero, zero))

    br = jnp.broadcast_to(br_ref[...], (nseq, ns))
    bi = jnp.broadcast_to(bi_ref[...], (nseq, ns))
    has_prev = lax.broadcasted_iota(jnp.int32, (nseq, ns), 0) % nsg >= 1
    i_re, i_im = zero, zero
    for _ in range(nsg - 1):
        t_re = f_re + br * i_re - bi * i_im
        t_im = f_im + br * i_im + bi * i_re
        i_re = jnp.where(has_prev, pltpu.roll(t_re, 1, axis=0), 0.0)
        i_im = jnp.where(has_prev, pltpu.roll(t_im, 1, axis=0), 0.0)

    def fix_step(n, carry):
        c_re, c_im = carry
        r0 = pl.multiple_of(n * nseq, nseq)
        s_ref[pl.ds(r0, nseq), 0:ns] += c_re
        s_ref[pl.ds(r0, nseq), ns:] += c_im
        return ar * c_re - ai * c_im, ar * c_im + ai * c_re

    lax.fori_loop(0, nl, fix_step, (i_re, i_im))

    ya = yall_ref[...] + lax.dot_general(s_ref[...].astype(BF16), wct_ref[LANES:(lc + 1) * LANES, :],
                                         (((1,), (1,)), ((), ())), preferred_element_type=F32)
    ya = ya.reshape(nl, nseq, lc * LANES)
    yt = jnp.stack([ya[:, :, j * LANES:(j + 1) * LANES] for j in range(lc)], axis=1)
    y_ref[...] = jnp.swapaxes(yt.reshape(tseg, nseq, LANES), 0, 1).reshape(nseq * tseg, LANES)


def _s5(proj, prm, bsz, t):
    bt = bsz * t
    nsg = SUBLANES // bsz
    tseg = t // nsg
    nl = tseg // S5_CHUNK
    r = nl * SUBLANES
    lc = S5_CHUNK
    ns = S5_SLAB_STATE
    n_slab = prm["bb_re"].shape[0] // S5_SLAB_GROUPS
    u_blk0 = PROJ_U_COL // LANES
    grp = lambda shape: pl.BlockSpec((S5_SLAB_GROUPS,) + shape, lambda s: (s,) + (0,) * len(shape))
    vec = lambda rows: pl.BlockSpec((rows, ns), lambda s: (0, s))
    return pl.pallas_call(
        functools.partial(_s5_body, nl=nl, nsg=nsg, tseg=tseg),
        grid=(n_slab,),
        in_specs=[
            pl.BlockSpec((bt, LANES), lambda s: (0, u_blk0 + s)),
            pl.BlockSpec((lc + 1, S5_SLAB_GROUPS, LANES), lambda s: (0, s, 0)),
            pl.BlockSpec((lc + 1, S5_SLAB_GROUPS, LANES), lambda s: (0, s, 0)),
            grp((S5_GROUP_CH, LANES)), grp((S5_GROUP_CH, LANES)), grp((S5_GROUP_CH, LANES)), grp((S5_GROUP_CH, LANES)),
            vec(1), vec(1), vec(1), vec(1),
        ],
        out_specs=pl.BlockSpec((bt, LANES), lambda s: (0, s)),
        out_shape=jax.ShapeDtypeStruct((bt, n_slab * LANES), F32),
        scratch_shapes=[
            pltpu.VMEM((lc * LANES, 2 * ns), BF16),
            pltpu.VMEM(((lc + 1) * LANES, 2 * ns), BF16),
            pltpu.VMEM((LANES, 2 * ns), BF16),
            pltpu.VMEM((lc * LANES, lc * LANES), BF16),
            pltpu.VMEM((r, 2 * ns), F32),
            pltpu.VMEM((r, 2 * ns), F32),
            pltpu.VMEM((r, lc * LANES), F32),
        ],
        compiler_params=_cparams("arbitrary"),
        name="s5",
    )(proj, prm["pw_re"], prm["pw_im"], prm["bb_re"], prm["bb_im"], prm["c_re"], prm["c_im"],
      prm["ac_re"], prm["ac_im"], prm["as_re"], prm["as_im"])


def _s5_params(a_re, a_im, log_dt, b_re, b_im, c_re, c_im, nl):
    lc = S5_CHUNK
    nlay, ng, npst = a_re.shape
    dt = jnp.exp(log_dt)[..., None]
    lam_re, lam_im = a_re * dt, a_im * dt

    def apow(n):
        n = n.astype(F32)[None, :, None, None]
        mag = jnp.exp(lam_re[:, None] * n)
        return mag * jnp.cos(lam_im[:, None] * n), mag * jnp.sin(lam_im[:, None] * n)

    ab_re, ab_im = jnp.exp(lam_re) * jnp.cos(lam_im), jnp.exp(lam_re) * jnp.sin(lam_im)
    den = a_re * a_re + a_im * a_im
    f_re = ((ab_re - 1.0) * a_re + ab_im * a_im) / den
    f_im = (ab_im * a_re - (ab_re - 1.0) * a_im) / den
    bb_re = (f_re[..., None] * b_re - f_im[..., None] * b_im).transpose(0, 1, 3, 2)
    bb_im = (f_re[..., None] * b_im + f_im[..., None] * b_re).transpose(0, 1, 3, 2)
    odd = (jnp.arange(ng) % 2 == 1)[None, :, None, None]

    def half(x):
        z = jnp.zeros_like(x)
        return jnp.concatenate([jnp.where(odd, z, x), jnp.where(odd, x, z)], axis=-1)

    both = lambda x: jnp.concatenate([x, x], axis=-1)
    flat = lambda x: x.reshape(nlay, x.shape[1], ng * npst)
    pw_re, pw_im = apow(jnp.arange(lc + 1))
    ac_re, ac_im = apow(jnp.array([lc]))
    as_re, as_im = apow(jnp.array([lc * nl]))
    return {
        "pw_re": both(pw_re), "pw_im": both(pw_im),
        "bb_re": half(bb_re), "bb_im": half(bb_im), "c_re": half(c_re), "c_im": half(c_im),
        "ac_re": flat(ac_re), "ac_im": flat(ac_im), "as_re": flat(as_re), "as_im": flat(as_im),
    }


def _outproj_body(m_ref, y_ref, u_ref, d_ref, wg_ref, bg_ref, wm_ref, ws_ref, x_ref, g_ref, xo_ref, ho_ref, *, ts):
    for r0 in range(0, m_ref.shape[0], ts):
        rows = slice(r0, r0 + ts)
        z = jax.nn.gelu(y_ref[rows, :] + d_ref[...] * u_ref[rows, :])
        gate = jnp.dot(z.astype(BF16), wg_ref[...], preferred_element_type=F32) + bg_ref[...]
        s_out = (z * jax.nn.sigmoid(gate)).astype(BF16)
        acc = jnp.dot(m_ref[rows, :], wm_ref[...], preferred_element_type=F32)
        acc = acc + jnp.dot(s_out, ws_ref[...], preferred_element_type=F32)
        x = x_ref[rows, :] + acc
        xo_ref[rows, :] = x
        ms = jnp.mean(x * x, axis=-1, keepdims=True)
        ho_ref[rows, :] = (x * lax.rsqrt(ms + EPS) * g_ref[...]).astype(ho_ref.dtype)


def _outproj(m_out, y, proj, s5_d, w_glu, b_glu, w, l, x, g, tm=512, ts=256):
    m, d = x.shape
    kh = m_out.shape[1]
    u_blk = PROJ_U_COL // kh
    const = lambda shape: pl.BlockSpec(shape, lambda i: (0,) * len(shape))
    resident = dict(pipeline_mode=pl.Buffered(1))
    return pl.pallas_call(
        functools.partial(_outproj_body, ts=ts),
        grid=(m // tm,),
        in_specs=[
            pl.BlockSpec((tm, kh), lambda i: (i, 0)),
            pl.BlockSpec((tm, kh), lambda i: (i, 0)),
            pl.BlockSpec((tm, kh), lambda i: (i, u_blk)),
            const((1, kh)),
            pl.BlockSpec((None, kh, kh), lambda i: (l, 0, 0), **resident),
            const((1, kh)),
            pl.BlockSpec((None, kh, d), lambda i: (l, 0, 0), **resident),
            pl.BlockSpec((None, kh, d), lambda i: (l, 1, 0), **resident),
            pl.BlockSpec((tm, d), lambda i: (i, 0)),
            const((1, d)),
        ],
        out_specs=[pl.BlockSpec((tm, d), lambda i: (i, 0)), pl.BlockSpec((tm, d), lambda i: (i, 0))],
        out_shape=[jax.ShapeDtypeStruct((m, d), F32), jax.ShapeDtypeStruct((m, d), BF16)],
        compiler_params=_cparams("parallel"),
        name="out_proj",
    )(m_out, y, proj, s5_d, w_glu, b_glu, w, w, x, g)


def _ffn_up_body(h_ref, wg_ref, wv_ref, cw_ref, cb_ref, a_ref, wb_ref, ext_ref, *, tm, ts, tiles_per_seq):
    i = pl.program_id(1)
    tn = a_ref.shape[1]

    @pl.when(i == 0)
    def _():
        wb_ref[:, :tn] = wg_ref[...].astype(BF16)
        wb_ref[:, tn:] = wv_ref[...].astype(BF16)

    @pl.when(i % tiles_per_seq == 0)
    def _():
        ext_ref[0:SUBLANES, :] = jnp.zeros((SUBLANES, tn), F32)

    for r0 in range(0, tm, ts):
        gv = jnp.dot(h_ref[r0:r0 + ts, :], wb_ref[...], preferred_element_type=F32)
        g = gv[:, :tn]
        ext_ref[SUBLANES + r0:SUBLANES + r0 + ts, :] = g
        conv = cb_ref[...] + cw_ref[F_CONV - 1:F_CONV, :] * g
        for kk in range(F_CONV - 1):
            conv = conv + cw_ref[kk:kk + 1, :] * ext_ref[pl.ds(SUBLANES + r0 - (F_CONV - 1) + kk, ts), :]
        a_ref[r0:r0 + ts, :] = (jax.nn.gelu(conv) * gv[:, tn:]).astype(a_ref.dtype)
    ext_ref[0:SUBLANES, :] = ext_ref[tm:tm + SUBLANES, :]


def _ffn_up(h, wg, wv, cw, cb, l, t, tm=2048, tn=512, ts=1024):
    m, k = h.shape
    n = wg.shape[2]
    tm = min(tm, t)
    ts = min(ts, tm)
    return pl.pallas_call(
        functools.partial(_ffn_up_body, tm=tm, ts=ts, tiles_per_seq=t // tm),
        grid=(pl.cdiv(n, tn), m // tm),
        in_specs=[
            pl.BlockSpec((tm, k), lambda j, i: (i, 0)),
            pl.BlockSpec((None, k, tn), lambda j, i: (l, 0, j)),
            pl.BlockSpec((None, k, tn), lambda j, i: (l, 0, j)),
            pl.BlockSpec((None, F_CONV, tn), lambda j, i: (l, 0, j)),
            pl.BlockSpec((None, 1, tn), lambda j, i: (l, 0, j)),
        ],
        out_specs=pl.BlockSpec((tm, tn), lambda j, i: (i, j)),
        out_shape=jax.ShapeDtypeStruct((m, n), BF16),
        scratch_shapes=[
            pltpu.VMEM((k, 2 * tn), BF16),
            pltpu.VMEM((tm + SUBLANES, tn), F32),
        ],
        compiler_params=_cparams("parallel", "arbitrary"),
        name="ffn_up",
    )(h, wg, wv, cw, cb)


def _ffn_down_body(a_ref, w_ref, x_ref, xo_ref, wb_ref):
    @pl.when(pl.program_id(1) == 0)
    def _():
        wb_ref[...] = w_ref[...].astype(BF16)

    xo_ref[...] = x_ref[...] + jnp.dot(a_ref[...], wb_ref[...], preferred_element_type=F32)


def _ffn_down(a, w, l, x, tm=512, tn=512):
    m, d = x.shape
    k = a.shape[1]
    return pl.pallas_call(
        _ffn_down_body,
        grid=(d // tn, m // tm),
        in_specs=[
            pl.BlockSpec((tm, k), lambda j, i: (i, 0)),
            pl.BlockSpec((None, k, tn), lambda j, i: (l, 0, j)),
            pl.BlockSpec((tm, tn), lambda j, i: (i, j)),
        ],
        out_specs=pl.BlockSpec((tm, tn), lambda j, i: (i, j)),
        out_shape=jax.ShapeDtypeStruct((m, d), F32),
        scratch_shapes=[pltpu.VMEM((k, tn), BF16)],
        compiler_params=_cparams("parallel", "arbitrary"),
        name="ffn_down",
    )(a, w, x)


def _pad_cols(w, mult):
    pad = (-w.shape[-1]) % mult
    return jnp.pad(w, [(0, 0)] * (w.ndim - 1) + [(0, pad)])


def _trunk(x, norm_mix_g, w_in, m_conv_w, m_conv_b, w_q, w_k, m_i_bias, m_f_bias, m_head_g, m_skip,
           s5_a_re, s5_a_im, s5_log_dt, s5_b_re, s5_b_im, s5_c_re, s5_c_im, s5_d, s5_w_glu, s5_b_glu,
           w_out, norm_ffn_g, w_gate, w_val, f_conv_w, f_conv_b, w_down, norm_final_g):
    bsz, t, d = x.shape
    depth = w_in.shape[0]
    bt = bsz * t
    nl = t // (SUBLANES // bsz) // S5_CHUNK

    c3 = PROJ_U_COL
    w_tail = jnp.concatenate([
        w_in[:, :, c3 + 2 * M_HEADS:],
        _pad_cols(w_in[:, :, c3:c3 + M_HEADS], LANES), _pad_cols(w_in[:, :, c3 + M_HEADS:c3 + 2 * M_HEADS], LANES),
    ], axis=-1).astype(BF16)
    w_head = w_in.astype(BF16)
    ib = _pad_cols(m_i_bias, LANES)[:, None, :]
    fb = _pad_cols(m_f_bias, LANES)[:, None, :]
    wq_b, wk_b = w_q.astype(BF16), w_k.astype(BF16)
    s5p = _s5_params(s5_a_re, s5_a_im, s5_log_dt, s5_b_re, s5_b_im, s5_c_re, s5_c_im, nl)
    wglu_b = s5_w_glu.astype(BF16)
    wout_b = w_out.astype(BF16)

    xf = x.reshape(bt, d)
    for l in range(depth):
        proj = _in_proj(xf, norm_mix_g[l][None], w_head, w_tail, l)
        m_out = _mlstm(proj, m_conv_w[l], m_conv_b[l][None], wq_b[l], wk_b[l], ib[l], fb[l],
                       m_head_g[l][None], m_skip[l][None], bsz, t)
        y = _s5(proj, {k_: v_[l] for k_, v_ in s5p.items()}, bsz, t)
        xf, h = _outproj(m_out, y, proj, s5_d[l][None], wglu_b, s5_b_glu[l][None], wout_b, l, xf, norm_ffn_g[l][None])
        a = _ffn_up(h, w_gate, w_val, f_conv_w, f_conv_b[:, None, :], l, t)
        xf = _ffn_down(a, w_down, l, xf)
    return _rmsnorm(xf, norm_final_g, F32).reshape(bsz, t, d)


def kernel(x, norm_mix_g, w_in, m_conv_w, m_conv_b, w_q, w_k, m_i_bias, m_f_bias, m_head_g, m_skip, s5_a_re, s5_a_im, s5_log_dt, s5_b_re, s5_b_im, s5_c_re, s5_c_im, s5_d, s5_w_glu, s5_b_glu, w_out, norm_ffn_g, w_gate, w_val, f_conv_w, f_conv_b, w_down, norm_final_g):
    return _trunk(x, norm_mix_g, w_in, m_conv_w, m_conv_b, w_q, w_k, m_i_bias, m_f_bias, m_head_g, m_skip,
                  s5_a_re, s5_a_im, s5_log_dt, s5_b_re, s5_b_im, s5_c_re, s5_c_im, s5_d, s5_w_glu, s5_b_glu,
                  w_out, norm_ffn_g, w_gate, w_val, f_conv_w, f_conv_b, w_down, norm_final_g)
```

```python
import functools

import jax
import jax.numpy as jnp
from jax import lax
from jax.experimental import pallas as pl
from jax.experimental.pallas import tpu as pltpu

F32 = jnp.float32
BF16 = jnp.bfloat16
EPS = 1e-6

M_HEADS = 4
M_HEAD_V = 256
M_HEAD_QK = 128
M_WIDTH = M_HEADS * M_HEAD_V
M_CONV = 4
M_CHUNK = 128
S5_GROUP_CH = 16
S5_STATE = 64
S5_CH = 1024
S5_CHUNK = 8
F_CONV = 3

LANES = 128
SUBLANES = 8
PROJ_U_COL = 3 * M_WIDTH
PROJ_GATE_COL = PROJ_U_COL + S5_CH
PROJ_COLS = PROJ_GATE_COL + 2 * LANES
S5_SLAB_GROUPS = LANES // S5_GROUP_CH
S5_SLAB_STATE = S5_SLAB_GROUPS * S5_STATE
VMEM_LIMIT = 56 * 1024 * 1024


def _cparams(*sem):
    return pltpu.CompilerParams(dimension_semantics=sem, vmem_limit_bytes=VMEM_LIMIT)


def _rmsnorm_body(x_ref, g_ref, o_ref):
    x = x_ref[...]
    ms = jnp.mean(x * x, axis=-1, keepdims=True)
    o_ref[...] = (x * lax.rsqrt(ms + EPS) * g_ref[...]).astype(o_ref.dtype)


def _rmsnorm(x, g, out_dtype, tm=512):
    m, d = x.shape
    return pl.pallas_call(
        _rmsnorm_body,
        grid=(m // tm,),
        in_specs=[pl.BlockSpec((tm, d), lambda i: (i, 0)), pl.BlockSpec((1, d), lambda i: (0, 0))],
        out_specs=pl.BlockSpec((tm, d), lambda i: (i, 0)),
        out_shape=jax.ShapeDtypeStruct((m, d), out_dtype),
        compiler_params=_cparams("parallel"),
        name="rmsnorm",
    )(x, g.reshape(1, d))


def _in_proj_body(x_ref, g_ref, wh_ref, wt_ref, o_ref, *, ts):
    n_head = wh_ref.shape[1]
    for r0 in range(0, x_ref.shape[0], ts):
        x = x_ref[r0:r0 + ts, :]
        ms = jnp.mean(x * x, axis=-1, keepdims=True)
        h = (x * lax.rsqrt(ms + EPS) * g_ref[...]).astype(BF16)
        o_ref[r0:r0 + ts, :n_head] = jnp.dot(h, wh_ref[...], preferred_element_type=F32)
        o_ref[r0:r0 + ts, n_head:] = jnp.dot(h, wt_ref[...], preferred_element_type=F32)


def _in_proj(x, g, w_head, w_tail, l, tm=512, ts=256):
    m, k = x.shape
    n_head, n_tail = PROJ_U_COL, w_tail.shape[2]
    resident = dict(pipeline_mode=pl.Buffered(1))
    return pl.pallas_call(
        functools.partial(_in_proj_body, ts=ts),
        grid=(m // tm,),
        in_specs=[
            pl.BlockSpec((tm, k), lambda i: (i, 0)),
            pl.BlockSpec((1, k), lambda i: (0, 0)),
            pl.BlockSpec((None, k, n_head), lambda i: (l, 0, 0), **resident),
            pl.BlockSpec((None, k, n_tail), lambda i: (l, 0, 0), **resident),
        ],
        out_specs=pl.BlockSpec((tm, n_head + n_tail), lambda i: (i, 0)),
        out_shape=jax.ShapeDtypeStruct((m, n_head + n_tail), F32),
        compiler_params=_cparams("parallel"),
        name="in_proj",
    )(x, g, w_head, w_tail)


def _log_sigmoid(x):
    return jnp.minimum(x, 0.0) - jnp.log1p(jnp.exp(-jnp.abs(x)))


def _mlstm_body(xm_ref, v_ref, o_ref, ig_ref, fg_ref, cw_ref, cb_ref, wq_ref, wk_ref, ib_ref, fb_ref,
                hg_ref, sk_ref, out_ref, ext_ref, c_ref, n_ref, m_ref, h_ref, *, tb):
    ncb = tb // M_CHUNK
    L = M_CHUNK

    @pl.when(pl.program_id(1) == 0)
    def _():
        ext_ref[0:SUBLANES, :] = jnp.zeros((SUBLANES, M_WIDTH), F32)
        c_ref[...] = jnp.zeros_like(c_ref)
        n_ref[...] = jnp.zeros_like(n_ref)
        m_ref[...] = jnp.zeros_like(m_ref)

    ext_ref[SUBLANES:, :] = xm_ref[...]
    acc = cb_ref[...] + cw_ref[0:1, :] * ext_ref[pl.ds(SUBLANES - 3, tb), :]
    for kk in range(1, M_CONV):
        acc = acc + cw_ref[kk:kk + 1, :] * ext_ref[pl.ds(SUBLANES - 3 + kk, tb), :]
    c = acc * jax.nn.sigmoid(acc)
    ext_ref[0:SUBLANES, :] = xm_ref[tb - SUBLANES:tb, :]

    i_pre = ig_ref[...] + ib_ref[...]
    lf = _log_sigmoid(fg_ref[...] + fb_ref[...])
    pos = lax.broadcasted_iota(jnp.int32, (tb, LANES), 0) % L
    bc = lf
    s = 1
    while s < L:
        bc = bc + jnp.where(pos >= s, pltpu.roll(bc, s, axis=0), 0.0)
        s *= 2
    rowv = i_pre - bc
    a = rowv
    s = 1
    while s < L:
        a = jnp.where(pos >= s, jnp.maximum(a, pltpu.roll(a, s, axis=0)), a)
        s *= 2
    bc3 = bc.reshape(ncb, L, LANES)
    g = bc3[:, L - 1:L, :]
    dec3 = g - bc3 + i_pre.reshape(ncb, L, LANES)
    maxdec = jnp.max(dec3, axis=1, keepdims=True)
    m_run = m_ref[0:1, :]
    m0_l, m1_l = [], []
    for nn in range(ncb):
        m0_l.append(m_run)
        m_run = jnp.maximum(g[nn] + m_run, maxdec[nn])
        m1_l.append(m_run)
    m_ref[0:1, :] = m_run
    m0 = jnp.stack(m0_l, axis=0)
    m1 = jnp.stack(m1_l, axis=0)
    mx = jnp.maximum(m0, a.reshape(ncb, L, LANES))
    colv = -mx
    w_inter = jnp.exp(m0 - mx)
    e_negm = jnp.exp(-(bc3 + mx))
    w_k = jnp.exp(dec3 - m1)
    w_c = jnp.exp(g + m0 - m1)
    rowv_t = rowv.T

    causal = lax.broadcasted_iota(jnp.int32, (L, L), 1) <= lax.broadcasted_iota(jnp.int32, (L, L), 0)
    scale = M_HEAD_QK ** -0.5

    for h in range(M_HEADS):
        c_h = c[:, h * M_HEAD_V:(h + 1) * M_HEAD_V].astype(BF16)
        q = jnp.dot(c_h, wq_ref[h], preferred_element_type=F32) * scale
        k = jnp.dot(c_h, wk_ref[h], preferred_element_type=F32)
        qb = q.astype(BF16)
        kb = k.astype(BF16)
        vb = v_ref[:, h * M_HEAD_V:(h + 1) * M_HEAD_V].astype(BF16)
        c_st = c_ref[h]
        n_st = n_ref[h]
        for nn in range(ncb):
            r0 = nn * L
            q_n = q[r0:r0 + L]
            qb_n = qb[r0:r0 + L]
            sc = lax.dot_general(qb_n, kb[r0:r0 + L], (((1,), (1,)), ((), ())),
                                 preferred_element_type=F32)
            arg = colv[nn][:, h:h + 1] + rowv_t[h:h + 1, r0:r0 + L]
            sw = sc * jnp.exp(jnp.where(causal, arg, -jnp.inf))
            wi = w_inter[nn][:, h:h + 1]
            num = jnp.dot(sw.astype(BF16), vb[r0:r0 + L], preferred_element_type=F32)
            num = num + wi * jnp.dot(qb_n, c_st.astype(BF16), preferred_element_type=F32)
            den = jnp.sum(sw, axis=1, keepdims=True) + wi * jnp.sum(q_n * n_st, axis=1, keepdims=True)
            den = jnp.maximum(jnp.abs(den), e_negm[nn][:, h:h + 1])
            h_ref[r0:r0 + L, h * M_HEAD_V:(h + 1) * M_HEAD_V] = num / den
            kw = k[r0:r0 + L] * w_k[nn][:, h:h + 1]
            wc = w_c[nn][:, h:h + 1]
            upd = lax.dot_general(kw.astype(BF16), vb[r0:r0 + L], (((0,), (0,)), ((), ())),
                                  preferred_element_type=F32)
            c_st = wc * c_st + upd
            n_st = wc * n_st + jnp.sum(kw, axis=0, keepdims=True)
        c_ref[h] = c_st
        n_ref[h] = n_st

    for h in range(M_HEADS):
        sl = slice(h * M_HEAD_V, (h + 1) * M_HEAD_V)
        hh = h_ref[:, sl]
        hn = hh * lax.rsqrt(jnp.mean(hh * hh, axis=-1, keepdims=True) + EPS) * hg_ref[:, sl]
        out = jax.nn.sigmoid(o_ref[:, sl]) * (hn + sk_ref[:, sl] * c[:, sl])
        out_ref[:, sl] = out.astype(out_ref.dtype)


def _mlstm(proj, cw, cb, wq, wk, ib, fb, hg, sk, bsz, t, tb=512):
    nt = t // tb
    gate_blk = PROJ_GATE_COL // LANES
    row = lambda b, i: b * nt + i
    full = lambda shape: pl.BlockSpec(shape, lambda b, i: (0,) * len(shape))
    return pl.pallas_call(
        functools.partial(_mlstm_body, tb=tb),
        grid=(bsz, nt),
        in_specs=[
            pl.BlockSpec((tb, M_WIDTH), lambda b, i: (row(b, i), 0)),
            pl.BlockSpec((tb, M_WIDTH), lambda b, i: (row(b, i), 1)),
            pl.BlockSpec((tb, M_WIDTH), lambda b, i: (row(b, i), 2)),
            pl.BlockSpec((tb, LANES), lambda b, i: (row(b, i), gate_blk)),
            pl.BlockSpec((tb, LANES), lambda b, i: (row(b, i), gate_blk + 1)),
            full((M_CONV, M_WIDTH)), full((1, M_WIDTH)),
            full((M_HEADS, M_HEAD_V, M_HEAD_QK)), full((M_HEADS, M_HEAD_V, M_HEAD_QK)),
            full((1, LANES)), full((1, LANES)), full((1, M_WIDTH)), full((1, M_WIDTH)),
        ],
        out_specs=pl.BlockSpec((tb, M_WIDTH), lambda b, i: (row(b, i), 0)),
        out_shape=jax.ShapeDtypeStruct((bsz * t, M_WIDTH), BF16),
        scratch_shapes=[
            pltpu.VMEM((tb + SUBLANES, M_WIDTH), F32),
            pltpu.VMEM((M_HEADS, M_HEAD_QK, M_HEAD_V), F32),
            pltpu.VMEM((M_HEADS, 1, M_HEAD_QK), F32),
            pltpu.VMEM((SUBLANES, LANES), F32),
            pltpu.VMEM((tb, M_WIDTH), F32),
        ],
        compiler_params=_cparams("parallel", "arbitrary"),
        name="mlstm",
    )(proj, proj, proj, proj, proj, cw, cb, wq, wk, ib, fb, hg, sk)


def _s5_body(u_ref, pre_ref, pim_ref, bre_ref, bim_ref, cre_ref, cim_ref, ar_ref, ai_ref, br_ref, bi_ref,
             y_ref, wst_ref, wct_ref, bbr_ref, toe_ref, v_ref, s_ref, yall_ref,
             *, nl, nsg, tseg):
    lc = S5_CHUNK
    ns = S5_SLAB_STATE
    gch = S5_GROUP_CH
    nseq = SUBLANES

    @pl.when(pl.program_id(0) == 0)
    def _():
        wst_ref[...] = jnp.zeros_like(wst_ref)
        wct_ref[...] = jnp.zeros_like(wct_ref)
        bbr_ref[...] = jnp.zeros_like(bbr_ref)
        toe_ref[...] = jnp.zeros_like(toe_ref)

    for gg in range(S5_SLAB_GROUPS):
        k0 = (gg // 2) * LANES
        bre, bim, cre, cim = bre_ref[gg], bim_ref[gg], cre_ref[gg], cim_ref[gg]
        r0 = gg * gch
        bbr_ref[r0:r0 + gch, k0:k0 + LANES] = bre.astype(BF16)
        bbr_ref[r0:r0 + gch, ns + k0:ns + k0 + LANES] = bim.astype(BF16)
        for j in range(lc + 1):
            p_re = pre_ref[j, gg:gg + 1, :]
            p_im = pim_ref[j, gg:gg + 1, :]
            r1 = j * LANES + r0
            wct_ref[r1:r1 + gch, k0:k0 + LANES] = (cre * p_re - cim * p_im).astype(BF16)
            wct_ref[r1:r1 + gch, ns + k0:ns + k0 + LANES] = (-(cre * p_im + cim * p_re)).astype(BF16)
            if j < lc:
                r2 = (lc - 1 - j) * LANES + r0
                wst_ref[r2:r2 + gch, k0:k0 + LANES] = (bre * p_re - bim * p_im).astype(BF16)
                wst_ref[r2:r2 + gch, ns + k0:ns + k0 + LANES] = (bre * p_im + bim * p_re).astype(BF16)
    kt = lax.dot_general(bbr_ref[...], wct_ref[0:lc * LANES, :], (((1,), (1,)), ((), ())),
                         preferred_element_type=F32).astype(BF16)
    for i in range(lc):
        toe_ref[i * LANES:(i + 1) * LANES, i * LANES:] = kt[:, :(lc - i) * LANES]

    ut = jnp.swapaxes(u_ref[...].reshape(nseq, tseg, LANES), 0, 1).reshape(nl, lc, nseq, LANES)
    uc = jnp.concatenate([ut[:, i] for i in range(lc)], axis=-1).reshape(nl * nseq, lc * LANES).astype(BF16)
    yall_ref[...] = jnp.dot(uc, toe_ref[...], preferred_element_type=F32)
    v_ref[...] = jnp.dot(uc, wst_ref[...], preferred_element_type=F32)

    ar = jnp.broadcast_to(ar_ref[...], (nseq, ns))
    ai = jnp.broadcast_to(ai_ref[...], (nseq, ns))

    def scan_step(n, carry):
        s_re, s_im = carry
        r0 = pl.multiple_of(n * nseq, nseq)
        s_ref[pl.ds(r0, nseq), 0:ns] = s_re
        s_ref[pl.ds(r0, nseq), ns:] = s_im
        n_re = ar * s_re - ai * s_im + v_ref[pl.ds(r0, nseq), 0:ns]
        n_im = ar * s_im + ai * s_re + v_ref[pl.ds(r0, nseq), ns:]
        return n_re, n_im

    zero = jnp.zeros((nseq, ns), F32)
    f_re, f_im = lax.fori_loop(0, nl, scan_step, (zero, zero))

    br = jnp.broadcast_to(br_ref[...], (nseq, ns))
    bi = jnp.broadcast_to(bi_ref[...], (nseq, ns))
    has_prev = lax.broadcasted_iota(jnp.int32, (nseq, ns), 0) % nsg >= 1
    i_re, i_im = zero, zero
    for _ in range(nsg - 1):
        t_re = f_re + br * i_re - bi * i_im
        t_im = f_im + br * i_im + bi * i_re
        i_re = jnp.where(has_prev, pltpu.roll(t_re, 1, axis=0), 0.0)
        i_im = jnp.where(has_prev, pltpu.roll(t_im, 1, axis=0), 0.0)

    def fix_step(n, carry):
        c_re, c_im = carry
        r0 = pl.multiple_of(n * nseq, nseq)
        s_ref[pl.ds(r0, nseq), 0:ns] += c_re
        s_ref[pl.ds(r0, nseq), ns:] += c_im
        return ar * c_re - ai * c_im, ar * c_im + ai * c_re

    lax.fori_loop(0, nl, fix_step, (i_re, i_im))

    ya = yall_ref[...] + lax.dot_general(s_ref[...].astype(BF16), wct_ref[LANES:(lc + 1) * LANES, :],
                                         (((1,), (1,)), ((), ())), preferred_element_type=F32)
    ya = ya.reshape(nl, nseq, lc * LANES)
    yt = jnp.stack([ya[:, :, j * LANES:(j + 1) * LANES] for j in range(lc)], axis=1)
    y_ref[...] = jnp.swapaxes(yt.reshape(tseg, nseq, LANES), 0, 1).reshape(nseq * tseg, LANES)


def _s5(proj, prm, bsz, t):
    bt = bsz * t
    nsg = SUBLANES // bsz
    tseg = t // nsg
    nl = tseg // S5_CHUNK
    r = nl * SUBLANES
    lc = S5_CHUNK
    ns = S5_SLAB_STATE
    n_slab = prm["bb_re"].shape[0] // S5_SLAB_GROUPS
    u_blk0 = PROJ_U_COL // LANES
    grp = lambda shape: pl.BlockSpec((S5_SLAB_GROUPS,) + shape, lambda s: (s,) + (0,) * len(shape))
    vec = lambda rows: pl.BlockSpec((rows, ns), lambda s: (0, s))
    return pl.pallas_call(
        functools.partial(_s5_body, nl=nl, nsg=nsg, tseg=tseg),
        grid=(n_slab,),
        in_specs=[
            pl.BlockSpec((bt, LANES), lambda s: (0, u_blk0 + s)),
            pl.BlockSpec((lc + 1, S5_SLAB_GROUPS, LANES), lambda s: (0, s, 0)),
            pl.BlockSpec((lc + 1, S5_SLAB_GROUPS, LANES), lambda s: (0, s, 0)),
            grp((S5_GROUP_CH, LANES)), grp((S5_GROUP_CH, LANES)), grp((S5_GROUP_CH, LANES)), grp((S5_GROUP_CH, LANES)),
            vec(1), vec(1), vec(1), vec(1),
        ],
        out_specs=pl.BlockSpec((bt, LANES), lambda s: (0, s)),
        out_shape=jax.ShapeDtypeStruct((bt, n_slab * LANES), F32),
        scratch_shapes=[
            pltpu.VMEM((lc * LANES, 2 * ns), BF16),
            pltpu.VMEM(((lc + 1) * LANES, 2 * ns), BF16),
            pltpu.VMEM((LANES, 2 * ns), BF16),
            pltpu.VMEM((lc * LANES, lc * LANES), BF16),
            pltpu.VMEM((r, 2 * ns), F32),
            pltpu.VMEM((r, 2 * ns), F32),
            pltpu.VMEM((r, lc * LANES), F32),
        ],
        compiler_params=_cparams("arbitrary"),
        name="s5",
    )(proj, prm["pw_re"], prm["pw_im"], prm["bb_re"], prm["bb_im"], prm["c_re"], prm["c_im"],
      prm["ac_re"], prm["ac_im"], prm["as_re"], prm["as_im"])


def _s5_params(a_re, a_im, log_dt, b_re, b_im, c_re, c_im, nl):
    lc = S5_CHUNK
    nlay, ng, npst = a_re.shape
    dt = jnp.exp(log_dt)[..., None]
    lam_re, lam_im = a_re * dt, a_im * dt

    def apow(n):
        n = n.astype(F32)[None, :, None, None]
        mag = jnp.exp(lam_re[:, None] * n)
        return mag * jnp.cos(lam_im[:, None] * n), mag * jnp.sin(lam_im[:, None] * n)

    ab_re, ab_im = jnp.exp(lam_re) * jnp.cos(lam_im), jnp.exp(lam_re) * jnp.sin(lam_im)
    den = a_re * a_re + a_im * a_im
    f_re = ((ab_re - 1.0) * a_re + ab_im * a_im) / den
    f_im = (ab_im * a_re - (ab_re - 1.0) * a_im) / den
    bb_re = (f_re[..., None] * b_re - f_im[..., None] * b_im).transpose(0, 1, 3, 2)
    bb_im = (f_re[..., None] * b_im + f_im[..., None] * b_re).transpose(0, 1, 3, 2)
    odd = (jnp.arange(ng) % 2 == 1)[None, :, None, None]

    def half(x):
        z = jnp.zeros_like(x)
        return jnp.concatenate([jnp.where(odd, z, x), jnp.where(odd, x, z)], axis=-1)

    both = lambda x: jnp.concatenate([x, x], axis=-1)
    flat = lambda x: x.reshape(nlay, x.shape[1], ng * npst)
    pw_re, pw_im = apow(jnp.arange(lc + 1))
    ac_re, ac_im = apow(jnp.array([lc]))
    as_re, as_im = apow(jnp.array([lc * nl]))
    return {
        "pw_re": both(pw_re), "pw_im": both(pw_im),
        "bb_re": half(bb_re), "bb_im": half(bb_im), "c_re": half(c_re), "c_im": half(c_im),
        "ac_re": flat(ac_re), "ac_im": flat(ac_im), "as_re": flat(as_re), "as_im": flat(as_im),
    }


def _outproj_body(m_ref, y_ref, u_ref, d_ref, wg_ref, bg_ref, wm_ref, ws_ref, x_ref, g_ref, xo_ref, ho_ref, *, ts):
    for r0 in range(0, m_ref.shape[0], ts):
        rows = slice(r0, r0 + ts)
        z = jax.nn.gelu(y_ref[rows, :] + d_ref[...] * u_ref[rows, :])
        gate = jnp.dot(z.astype(BF16), wg_ref[...], preferred_element_type=F32) + bg_ref[...]
        s_out = (z * jax.nn.sigmoid(gate)).astype(BF16)
        acc = jnp.dot(m_ref[rows, :], wm_ref[...], preferred_element_type=F32)
        acc = acc + jnp.dot(s_out, ws_ref[...], preferred_element_type=F32)
        x = x_ref[rows, :] + acc
        xo_ref[rows, :] = x
        ms = jnp.mean(x * x, axis=-1, keepdims=True)
        ho_ref[rows, :] = (x * lax.rsqrt(ms + EPS) * g_ref[...]).astype(ho_ref.dtype)


def _outproj(m_out, y, proj, s5_d, w_glu, b_glu, w, l, x, g, tm=512, ts=256):
    m, d = x.shape
    kh = m_out.shape[1]
    u_blk = PROJ_U_COL // kh
    const = lambda shape: pl.BlockSpec(shape, lambda i: (0,) * len(shape))
    resident = dict(pipeline_mode=pl.Buffered(1))
    return pl.pallas_call(
        functools.partial(_outproj_body, ts=ts),
        grid=(m // tm,),
        in_specs=[
            pl.BlockSpec((tm, kh), lambda i: (i, 0)),
            pl.BlockSpec((tm, kh), lambda i: (i, 0)),
            pl.BlockSpec((tm, kh), lambda i: (i, u_blk)),
            const((1, kh)),
            pl.BlockSpec((None, kh, kh), lambda i: (l, 0, 0), **resident),
            const((1, kh)),
            pl.BlockSpec((None, kh, d), lambda i: (l, 0, 0), **resident),
            pl.BlockSpec((None, kh, d), lambda i: (l, 1, 0), **resident),
            pl.BlockSpec((tm, d), lambda i: (i, 0)),
            const((1, d)),
        ],
        out_specs=[pl.BlockSpec((tm, d), lambda i: (i, 0)), pl.BlockSpec((tm, d), lambda i: (i, 0))],
        out_shape=[jax.ShapeDtypeStruct((m, d), F32), jax.ShapeDtypeStruct((m, d), BF16)],
        compiler_params=_cparams("parallel"),
        name="out_proj",
    )(m_out, y, proj, s5_d, w_glu, b_glu, w, w, x, g)


def _ffn_up_body(h_ref, wg_ref, wv_ref, cw_ref, cb_ref, a_ref, wb_ref, ext_ref, *, tm, ts, tiles_per_seq):
    i = pl.program_id(1)
    tn = a_ref.shape[1]

    @pl.when(i == 0)
    def _():
        wb_ref[:, :tn] = wg_ref[...].astype(BF16)
        wb_ref[:, tn:] = wv_ref[...].astype(BF16)

    @pl.when(i % tiles_per_seq == 0)
    def _():
        ext_ref[0:SUBLANES, :] = jnp.zeros((SUBLANES, tn), F32)

    for r0 in range(0, tm, ts):
        gv = jnp.dot(h_ref[r0:r0 + ts, :], wb_ref[...], preferred_element_type=F32)
        g = gv[:, :tn]
        ext_ref[SUBLANES + r0:SUBLANES + r0 + ts, :] = g
        conv = cb_ref[...] + cw_ref[F_CONV - 1:F_CONV, :] * g
        for kk in range(F_CONV - 1):
            conv = conv + cw_ref[kk:kk + 1, :] * ext_ref[pl.ds(SUBLANES + r0 - (F_CONV - 1) + kk, ts), :]
        a_ref[r0:r0 + ts, :] = (jax.nn.gelu(conv) * gv[:, tn:]).astype(a_ref.dtype)
    ext_ref[0:SUBLANES, :] = ext_ref[tm:tm + SUBLANES, :]


def _ffn_up(h, wg, wv, cw, cb, l, t, tm=2048, tn=512, ts=1024):
    m, k = h.shape
    n = wg.shape[2]
    tm = min(tm, t)
    ts = min(ts, tm)
    return pl.pallas_call(
        functools.partial(_ffn_up_body, tm=tm, ts=ts, tiles_per_seq=t // tm),
        grid=(pl.cdiv(n, tn), m // tm),
        in_specs=[
            pl.BlockSpec((tm, k), lambda j, i: (i, 0)),
            pl.BlockSpec((None, k, tn), lambda j, i: (l, 0, j)),
            pl.BlockSpec((None, k, tn), lambda j, i: (l, 0, j)),
            pl.BlockSpec((None, F_CONV, tn), lambda j, i: (l, 0, j)),
            pl.BlockSpec((None, 1, tn), lambda j, i: (l, 0, j)),
        ],
        out_specs=pl.BlockSpec((tm, tn), lambda j, i: (i, j)),
        out_shape=jax.ShapeDtypeStruct((m, n), BF16),
        scratch_shapes=[
            pltpu.VMEM((k, 2 * tn), BF16),
            pltpu.VMEM((tm + SUBLANES, tn), F32),
        ],
        compiler_params=_cparams("parallel", "arbitrary"),
        name="ffn_up",
    )(h, wg, wv, cw, cb)


def _ffn_down_body(a_ref, w_ref, x_ref, xo_ref, wb_ref):
    @pl.when(pl.program_id(1) == 0)
    def _():
        wb_ref[...] = w_ref[...].astype(BF16)

    xo_ref[...] = x_ref[...] + jnp.dot(a_ref[...], wb_ref[...], preferred_element_type=F32)


def _ffn_down(a, w, l, x, tm=512, tn=512):
    m, d = x.shape
    k = a.shape[1]
    return pl.pallas_call(
        _ffn_down_body,
        grid=(d // tn, m // tm),
        in_specs=[
            pl.BlockSpec((tm, k), lambda j, i: (i, 0)),
            pl.BlockSpec((None, k, tn), lambda j, i: (l, 0, j)),
            pl.BlockSpec((tm, tn), lambda j, i: (i, j)),
        ],
        out_specs=pl.BlockSpec((tm, tn), lambda j, i: (i, j)),
        out_shape=jax.ShapeDtypeStruct((m, d), F32),
        scratch_shapes=[pltpu.VMEM((k, tn), BF16)],
        compiler_params=_cparams("parallel", "arbitrary"),
        name="ffn_down",
    )(a, w, x)


def _pad_cols(w, mult):
    pad = (-w.shape[-1]) % mult
    return jnp.pad(w, [(0, 0)] * (w.ndim - 1) + [(0, pad)])


def _trunk(x, norm_mix_g, w_in, m_conv_w, m_conv_b, w_q, w_k, m_i_bias, m_f_bias, m_head_g, m_skip,
           s5_a_re, s5_a_im, s5_log_dt, s5_b_re, s5_b_im, s5_c_re, s5_c_im, s5_d, s5_w_glu, s5_b_glu,
           w_out, norm_ffn_g, w_gate, w_val, f_conv_w, f_conv_b, w_down, norm_final_g):
    bsz, t, d = x.shape
    depth = w_in.shape[0]
    bt = bsz * t
    nl = t // (SUBLANES // bsz) // S5_CHUNK

    c3 = PROJ_U_COL
    w_tail = jnp.concatenate([
        w_in[:, :, c3 + 2 * M_HEADS:],
        _pad_cols(w_in[:, :, c3:c3 + M_HEADS], LANES), _pad_cols(w_in[:, :, c3 + M_HEADS:c3 + 2 * M_HEADS], LANES),
    ], axis=-1).astype(BF16)
    w_head = w_in.astype(BF16)
    ib = _pad_cols(m_i_bias, LANES)[:, None, :]
    fb = _pad_cols(m_f_bias, LANES)[:, None, :]
    wq_b, wk_b = w_q.astype(BF16), w_k.astype(BF16)
    s5p = _s5_params(s5_a_re, s5_a_im, s5_log_dt, s5_b_re, s5_b_im, s5_c_re, s5_c_im, nl)
    wglu_b = s5_w_glu.astype(BF16)
    wout_b = w_out.astype(BF16)

    xf = x.reshape(bt, d)
    for l in range(depth):
        proj = _in_proj(xf, norm_mix_g[l][None], w_head, w_tail, l)
        m_out = _mlstm(proj, m_conv_w[l], m_conv_b[l][None], wq_b[l], wk_b[l], ib[l], fb[l],
                       m_head_g[l][None], m_skip[l][None], bsz, t)
        y = _s5(proj, {k_: v_[l] for k_, v_ in s5p.items()}, bsz, t)
        xf, h = _outproj(m_out, y, proj, s5_d[l][None], wglu_b, s5_b_glu[l][None], wout_b, l, xf, norm_ffn_g[l][None])
        a = _ffn_up(h, w_gate, w_val, f_conv_w, f_conv_b[:, None, :], l, t)
        xf = _ffn_down(a, w_down, l, xf)
    return _rmsnorm(xf, norm_final_g, F32).reshape(bsz, t, d)


def kernel(x, norm_mix_g, w_in, m_conv_w, m_conv_b, w_q, w_k, m_i_bias, m_f_bias, m_head_g, m_skip, s5_a_re, s5_a_im, s5_log_dt, s5_b_re, s5_b_im, s5_c_re, s5_c_im, s5_d, s5_w_glu, s5_b_glu, w_out, norm_ffn_g, w_gate, w_val, f_conv_w, f_conv_b, w_down, norm_final_g):
    return _trunk(x, norm_mix_g, w_in, m_conv_w, m_conv_b, w_q, w_k, m_i_bias, m_f_bias, m_head_g, m_skip,
                  s5_a_re, s5_a_im, s5_log_dt, s5_b_re, s5_b_im, s5_c_re, s5_c_im, s5_d, s5_w_glu, s5_b_glu,
                  w_out, norm_ffn_g, w_gate, w_val, f_conv_w, f_conv_b, w_down, norm_final_g)
```

```python
import functools

import jax
import jax.numpy as jnp
from jax import lax
from jax.experimental import pallas as pl
from jax.experimental.pallas import tpu as pltpu

F32 = jnp.float32
BF16 = jnp.bfloat16
EPS = 1e-6

M_HEADS = 4
M_HEAD_V = 256
M_HEAD_QK = 128
M_WIDTH = M_HEADS * M_HEAD_V
M_CONV = 4
M_CHUNK = 128
S5_GROUP_CH = 16
S5_STATE = 64
S5_CH = 1024
S5_CHUNK = 8
F_CONV = 3

LANES = 128
SUBLANES = 8
PROJ_U_COL = 3 * M_WIDTH
PROJ_GATE_COL = PROJ_U_COL + S5_CH
PROJ_COLS = PROJ_GATE_COL + 2 * LANES
S5_SLAB_GROUPS = LANES // S5_GROUP_CH
S5_SLAB_STATE = S5_SLAB_GROUPS * S5_STATE
VMEM_LIMIT = 56 * 1024 * 1024
VMEM_SLACK = 1024 * 1024


def _cparams(*sem, vmem_limit=VMEM_LIMIT):
    return pltpu.CompilerParams(dimension_semantics=sem, vmem_limit_bytes=vmem_limit)


def _rmsnorm_body(x_ref, g_ref, o_ref):
    x = x_ref[...]
    ms = jnp.mean(x * x, axis=-1, keepdims=True)
    o_ref[...] = (x * lax.rsqrt(ms + EPS) * g_ref[...]).astype(o_ref.dtype)


def _rmsnorm(x, g, out_dtype, tm=512):
    m, d = x.shape
    return pl.pallas_call(
        _rmsnorm_body,
        grid=(m // tm,),
        in_specs=[pl.BlockSpec((tm, d), lambda i: (i, 0)), pl.BlockSpec((1, d), lambda i: (0, 0))],
        out_specs=pl.BlockSpec((tm, d), lambda i: (i, 0)),
        out_shape=jax.ShapeDtypeStruct((m, d), out_dtype),
        compiler_params=_cparams("parallel"),
        name="rmsnorm",
    )(x, g.reshape(1, d))


def _in_proj_body(x_ref, g_ref, wh_ref, wt_ref, o_ref, *, ts):
    n_head = wh_ref.shape[1]
    for r0 in range(0, x_ref.shape[0], ts):
        x = x_ref[r0:r0 + ts, :]
        ms = jnp.mean(x * x, axis=-1, keepdims=True)
        h = (x * lax.rsqrt(ms + EPS) * g_ref[...]).astype(BF16)
        o_ref[r0:r0 + ts, :n_head] = jnp.dot(h, wh_ref[...], preferred_element_type=F32)
        o_ref[r0:r0 + ts, n_head:] = jnp.dot(h, wt_ref[...], preferred_element_type=F32)


def _in_proj(x, g, w_head, w_tail, l, tm=512, ts=256):
    m, k = x.shape
    n_head, n_tail = PROJ_U_COL, w_tail.shape[2]
    resident = dict(pipeline_mode=pl.Buffered(1))
    return pl.pallas_call(
        functools.partial(_in_proj_body, ts=ts),
        grid=(m // tm,),
        in_specs=[
            pl.BlockSpec((tm, k), lambda i: (i, 0)),
            pl.BlockSpec((1, k), lambda i: (0, 0)),
            pl.BlockSpec((None, k, n_head), lambda i: (l, 0, 0), **resident),
            pl.BlockSpec((None, k, n_tail), lambda i: (l, 0, 0), **resident),
        ],
        out_specs=pl.BlockSpec((tm, n_head + n_tail), lambda i: (i, 0)),
        out_shape=jax.ShapeDtypeStruct((m, n_head + n_tail), F32),
        compiler_params=_cparams("parallel"),
        name="in_proj",
    )(x, g, w_head, w_tail)


def _log_sigmoid(x):
    return jnp.minimum(x, 0.0) - jnp.log1p(jnp.exp(-jnp.abs(x)))


def _mlstm_body(xm_ref, v_ref, o_ref, ig_ref, fg_ref, cw_ref, cb_ref, wq_ref, wk_ref, ib_ref, fb_ref,
                hg_ref, sk_ref, out_ref, ext_ref, c_ref, n_ref, m_ref, h_ref, *, tb):
    ncb = tb // M_CHUNK
    L = M_CHUNK

    @pl.when(pl.program_id(1) == 0)
    def _():
        ext_ref[0:SUBLANES, :] = jnp.zeros((SUBLANES, M_WIDTH), F32)
        c_ref[...] = jnp.zeros_like(c_ref)
        n_ref[...] = jnp.zeros_like(n_ref)
        m_ref[...] = jnp.zeros_like(m_ref)

    ext_ref[SUBLANES:, :] = xm_ref[...]
    acc = cb_ref[...] + cw_ref[0:1, :] * ext_ref[pl.ds(SUBLANES - 3, tb), :]
    for kk in range(1, M_CONV):
        acc = acc + cw_ref[kk:kk + 1, :] * ext_ref[pl.ds(SUBLANES - 3 + kk, tb), :]
    c = acc * jax.nn.sigmoid(acc)
    ext_ref[0:SUBLANES, :] = xm_ref[tb - SUBLANES:tb, :]

    i_pre = ig_ref[...] + ib_ref[...]
    lf = _log_sigmoid(fg_ref[...] + fb_ref[...])
    pos = lax.broadcasted_iota(jnp.int32, (tb, LANES), 0) % L
    bc = lf
    s = 1
    while s < L:
        bc = bc + jnp.where(pos >= s, pltpu.roll(bc, s, axis=0), 0.0)
        s *= 2
    rowv = i_pre - bc
    a = rowv
    s = 1
    while s < L:
        a = jnp.where(pos >= s, jnp.maximum(a, pltpu.roll(a, s, axis=0)), a)
        s *= 2
    bc3 = bc.reshape(ncb, L, LANES)
    g = bc3[:, L - 1:L, :]
    dec3 = g - bc3 + i_pre.reshape(ncb, L, LANES)
    maxdec = jnp.max(dec3, axis=1, keepdims=True)
    m_run = m_ref[0:1, :]
    m0_l, m1_l = [], []
    for nn in range(ncb):
        m0_l.append(m_run)
        m_run = jnp.maximum(g[nn] + m_run, maxdec[nn])
        m1_l.append(m_run)
    m_ref[0:1, :] = m_run
    m0 = jnp.stack(m0_l, axis=0)
    m1 = jnp.stack(m1_l, axis=0)
    mx = jnp.maximum(m0, a.reshape(ncb, L, LANES))
    colv = -mx
    w_inter = jnp.exp(m0 - mx)
    e_negm = jnp.exp(-(bc3 + mx))
    w_k = jnp.exp(dec3 - m1)
    w_c = jnp.exp(g + m0 - m1)
    rowv_t = rowv.T

    causal = lax.broadcasted_iota(jnp.int32, (L, L), 1) <= lax.broadcasted_iota(jnp.int32, (L, L), 0)
    scale = M_HEAD_QK ** -0.5

    for h in range(M_HEADS):
        c_h = c[:, h * M_HEAD_V:(h + 1) * M_HEAD_V].astype(BF16)
        q = jnp.dot(c_h, wq_ref[h], preferred_element_type=F32) * scale
        k = jnp.dot(c_h, wk_ref[h], preferred_element_type=F32)
        qb = q.astype(BF16)
        kb = k.astype(BF16)
        vb = v_ref[:, h * M_HEAD_V:(h + 1) * M_HEAD_V].astype(BF16)
        c_st = c_ref[h]
        n_st = n_ref[h]
        for nn in range(ncb):
            r0 = nn * L
            q_n = q[r0:r0 + L]
            qb_n = qb[r0:r0 + L]
            sc = lax.dot_general(qb_n, kb[r0:r0 + L], (((1,), (1,)), ((), ())),
                                 preferred_element_type=F32)
            arg = colv[nn][:, h:h + 1] + rowv_t[h:h + 1, r0:r0 + L]
            sw = sc * jnp.exp(jnp.where(causal, arg, -jnp.inf))
            wi = w_inter[nn][:, h:h + 1]
            num = jnp.dot(sw.astype(BF16), vb[r0:r0 + L], preferred_element_type=F32)
            num = num + wi * jnp.dot(qb_n, c_st.astype(BF16), preferred_element_type=F32)
            den = jnp.sum(sw, axis=1, keepdims=True) + wi * jnp.sum(q_n * n_st, axis=1, keepdims=True)
            den = jnp.maximum(jnp.abs(den), e_negm[nn][:, h:h + 1])
            h_ref[r0:r0 + L, h * M_HEAD_V:(h + 1) * M_HEAD_V] = num / den
            kw = k[r0:r0 + L] * w_k[nn][:, h:h + 1]
            wc = w_c[nn][:, h:h + 1]
            upd = lax.dot_general(kw.astype(BF16), vb[r0:r0 + L], (((0,), (0,)), ((), ())),
                                  preferred_element_type=F32)
            c_st = wc * c_st + upd
            n_st = wc * n_st + jnp.sum(kw, axis=0, keepdims=True)
        c_ref[h] = c_st
        n_ref[h] = n_st

    for h in range(M_HEADS):
        sl = slice(h * M_HEAD_V, (h + 1) * M_HEAD_V)
        hh = h_ref[:, sl]
        hn = hh * lax.rsqrt(jnp.mean(hh * hh, axis=-1, keepdims=True) + EPS) * hg_ref[:, sl]
        out = jax.nn.sigmoid(o_ref[:, sl]) * (hn + sk_ref[:, sl] * c[:, sl])
        out_ref[:, sl] = out.astype(out_ref.dtype)


def _mlstm(proj, cw, cb, wq, wk, ib, fb, hg, sk, bsz, t, tb=512):
    nt = t // tb
    gate_blk = PROJ_GATE_COL // LANES
    row = lambda b, i: b * nt + i
    full = lambda shape: pl.BlockSpec(shape, lambda b, i: (0,) * len(shape))
    return pl.pallas_call(
        functools.partial(_mlstm_body, tb=tb),
        grid=(bsz, nt),
        in_specs=[
            pl.BlockSpec((tb, M_WIDTH), lambda b, i: (row(b, i), 0)),
            pl.BlockSpec((tb, M_WIDTH), lambda b, i: (row(b, i), 1)),
            pl.BlockSpec((tb, M_WIDTH), lambda b, i: (row(b, i), 2)),
            pl.BlockSpec((tb, LANES), lambda b, i: (row(b, i), gate_blk)),
            pl.BlockSpec((tb, LANES), lambda b, i: (row(b, i), gate_blk + 1)),
            full((M_CONV, M_WIDTH)), full((1, M_WIDTH)),
            full((M_HEADS, M_HEAD_V, M_HEAD_QK)), full((M_HEADS, M_HEAD_V, M_HEAD_QK)),
            full((1, LANES)), full((1, LANES)), full((1, M_WIDTH)), full((1, M_WIDTH)),
        ],
        out_specs=pl.BlockSpec((tb, M_WIDTH), lambda b, i: (row(b, i), 0)),
        out_shape=jax.ShapeDtypeStruct((bsz * t, M_WIDTH), BF16),
        scratch_shapes=[
            pltpu.VMEM((tb + SUBLANES, M_WIDTH), F32),
            pltpu.VMEM((M_HEADS, M_HEAD_QK, M_HEAD_V), F32),
            pltpu.VMEM((M_HEADS, 1, M_HEAD_QK), F32),
            pltpu.VMEM((SUBLANES, LANES), F32),
            pltpu.VMEM((tb, M_WIDTH), F32),
        ],
        compiler_params=_cparams("parallel", "arbitrary"),
        name="mlstm",
    )(proj, proj, proj, proj, proj, cw, cb, wq, wk, ib, fb, hg, sk)


def _s5_body(u_ref, pre_ref, pim_ref, bre_ref, bim_ref, cre_ref, cim_ref, ar_ref, ai_ref, br_ref, bi_ref,
             y_ref, wst_ref, wct_ref, bbr_ref, toe_ref, v_ref, s_ref, yall_ref,
             *, nl, nsg, tseg):
    lc = S5_CHUNK
    ns = S5_SLAB_STATE
    gch = S5_GROUP_CH
    nseq = SUBLANES

    @pl.when(pl.program_id(0) == 0)
    def _():
        wst_ref[...] = jnp.zeros_like(wst_ref)
        wct_ref[...] = jnp.zeros_like(wct_ref)
        bbr_ref[...] = jnp.zeros_like(bbr_ref)
        toe_ref[...] = jnp.zeros_like(toe_ref)

    for gg in range(S5_SLAB_GROUPS):
        k0 = (gg // 2) * LANES
        bre, bim, cre, cim = bre_ref[gg], bim_ref[gg], cre_ref[gg], cim_ref[gg]
        r0 = gg * gch
        bbr_ref[r0:r0 + gch, k0:k0 + LANES] = bre.astype(BF16)
        bbr_ref[r0:r0 + gch, ns + k0:ns + k0 + LANES] = bim.astype(BF16)
        for j in range(lc + 1):
            p_re = pre_ref[j, gg:gg + 1, :]
            p_im = pim_ref[j, gg:gg + 1, :]
            r1 = j * LANES + r0
            wct_ref[r1:r1 + gch, k0:k0 + LANES] = (cre * p_re - cim * p_im).astype(BF16)
            wct_ref[r1:r1 + gch, ns + k0:ns + k0 + LANES] = (-(cre * p_im + cim * p_re)).astype(BF16)
            if j < lc:
                r2 = (lc - 1 - j) * LANES + r0
                wst_ref[r2:r2 + gch, k0:k0 + LANES] = (bre * p_re - bim * p_im).astype(BF16)
                wst_ref[r2:r2 + gch, ns + k0:ns + k0 + LANES] = (bre * p_im + bim * p_re).astype(BF16)
    kt = lax.dot_general(bbr_ref[...], wct_ref[0:lc * LANES, :], (((1,), (1,)), ((), ())),
                         preferred_element_type=F32).astype(BF16)
    for i in range(lc):
        toe_ref[i * LANES:(i + 1) * LANES, i * LANES:] = kt[:, :(lc - i) * LANES]

    ut = jnp.swapaxes(u_ref[...].reshape(nseq, tseg, LANES), 0, 1).reshape(nl, lc, nseq, LANES)
    uc = jnp.concatenate([ut[:, i] for i in range(lc)], axis=-1).reshape(nl * nseq, lc * LANES).astype(BF16)
    yall_ref[...] = jnp.dot(uc, toe_ref[...], preferred_element_type=F32)
    v_ref[...] = jnp.dot(uc, wst_ref[...], preferred_element_type=F32)

    ar = jnp.broadcast_to(ar_ref[...], (nseq, ns))
    ai = jnp.broadcast_to(ai_ref[...], (nseq, ns))

    def scan_step(n, carry):
        s_re, s_im = carry
        r0 = pl.multiple_of(n * nseq, nseq)
        s_ref[pl.ds(r0, nseq), 0:ns] = s_re
        s_ref[pl.ds(r0, nseq), ns:] = s_im
        n_re = ar * s_re - ai * s_im + v_ref[pl.ds(r0, nseq), 0:ns]
        n_im = ar * s_im + ai * s_re + v_ref[pl.ds(r0, nseq), ns:]
        return n_re, n_im

    zero = jnp.zeros((nseq, ns), F32)
    f_re, f_im = lax.fori_loop(0, nl, scan_step, (zero, zero))

    br = jnp.broadcast_to(br_ref[...], (nseq, ns))
    bi = jnp.broadcast_to(bi_ref[...], (nseq, ns))
    has_prev = lax.broadcasted_iota(jnp.int32, (nseq, ns), 0) % nsg >= 1
    i_re, i_im = zero, zero
    for _ in range(nsg - 1):
        t_re = f_re + br * i_re - bi * i_im
        t_im = f_im + br * i_im + bi * i_re
        i_re = jnp.where(has_prev, pltpu.roll(t_re, 1, axis=0), 0.0)
        i_im = jnp.where(has_prev, pltpu.roll(t_im, 1, axis=0), 0.0)

    def fix_step(n, carry):
        c_re, c_im = carry
        r0 = pl.multiple_of(n * nseq, nseq)
        s_ref[pl.ds(r0, nseq), 0:ns] += c_re
        s_ref[pl.ds(r0, nseq), ns:] += c_im
        return ar * c_re - ai * c_im, ar * c_im + ai * c_re

    lax.fori_loop(0, nl, fix_step, (i_re, i_im))

    ya = yall_ref[...] + lax.dot_general(s_ref[...].astype(BF16), wct_ref[LANES:(lc + 1) * LANES, :],
                                         (((1,), (1,)), ((), ())), preferred_element_type=F32)
    ya = ya.reshape(nl, nseq, lc * LANES)
    yt = jnp.stack([ya[:, :, j * LANES:(j + 1) * LANES] for j in range(lc)], axis=1)
    y_ref[...] = jnp.swapaxes(yt.reshape(tseg, nseq, LANES), 0, 1).reshape(nseq * tseg, LANES)


def _s5(proj, prm, bsz, t):
    bt = bsz * t
    nsg = SUBLANES // bsz
    tseg = t // nsg
    nl = tseg // S5_CHUNK
    r = nl * SUBLANES
    lc = S5_CHUNK
    ns = S5_SLAB_STATE
    n_slab = prm["bb_re"].shape[0] // S5_SLAB_GROUPS
    u_blk0 = PROJ_U_COL // LANES
    grp = lambda shape: pl.BlockSpec((S5_SLAB_GROUPS,) + shape, lambda s: (s,) + (0,) * len(shape))
    vec = lambda rows: pl.BlockSpec((rows, ns), lambda s: (0, s))
    return pl.pallas_call(
        functools.partial(_s5_body, nl=nl, nsg=nsg, tseg=tseg),
        grid=(n_slab,),
        in_specs=[
            pl.BlockSpec((bt, LANES), lambda s: (0, u_blk0 + s)),
            pl.BlockSpec((lc + 1, S5_SLAB_GROUPS, LANES), lambda s: (0, s, 0)),
            pl.BlockSpec((lc + 1, S5_SLAB_GROUPS, LANES), lambda s: (0, s, 0)),
            grp((S5_GROUP_CH, LANES)), grp((S5_GROUP_CH, LANES)), grp((S5_GROUP_CH, LANES)), grp((S5_GROUP_CH, LANES)),
            vec(1), vec(1), vec(1), vec(1),
        ],
        out_specs=pl.BlockSpec((bt, LANES), lambda s: (0, s)),
        out_shape=jax.ShapeDtypeStruct((bt, n_slab * LANES), F32),
        scratch_shapes=[
            pltpu.VMEM((lc * LANES, 2 * ns), BF16),
            pltpu.VMEM(((lc + 1) * LANES, 2 * ns), BF16),
            pltpu.VMEM((LANES, 2 * ns), BF16),
            pltpu.VMEM((lc * LANES, lc * LANES), BF16),
            pltpu.VMEM((r, 2 * ns), F32),
            pltpu.VMEM((r, 2 * ns), F32),
            pltpu.VMEM((r, lc * LANES), F32),
        ],
        compiler_params=_cparams("arbitrary"),
        name="s5",
    )(proj, prm["pw_re"], prm["pw_im"], prm["bb_re"], prm["bb_im"], prm["c_re"], prm["c_im"],
      prm["ac_re"], prm["ac_im"], prm["as_re"], prm["as_im"])


def _s5_params(a_re, a_im, log_dt, b_re, b_im, c_re, c_im, nl):
    lc = S5_CHUNK
    nlay, ng, npst = a_re.shape
    dt = jnp.exp(log_dt)[..., None]
    lam_re, lam_im = a_re * dt, a_im * dt

    def apow(n):
        n = n.astype(F32)[None, :, None, None]
        mag = jnp.exp(lam_re[:, None] * n)
        return mag * jnp.cos(lam_im[:, None] * n), mag * jnp.sin(lam_im[:, None] * n)

    ab_re, ab_im = jnp.exp(lam_re) * jnp.cos(lam_im), jnp.exp(lam_re) * jnp.sin(lam_im)
    den = a_re * a_re + a_im * a_im
    f_re = ((ab_re - 1.0) * a_re + ab_im * a_im) / den
    f_im = (ab_im * a_re - (ab_re - 1.0) * a_im) / den
    bb_re = (f_re[..., None] * b_re - f_im[..., None] * b_im).transpose(0, 1, 3, 2)
    bb_im = (f_re[..., None] * b_im + f_im[..., None] * b_re).transpose(0, 1, 3, 2)
    odd = (jnp.arange(ng) % 2 == 1)[None, :, None, None]

    def half(x):
        z = jnp.zeros_like(x)
        return jnp.concatenate([jnp.where(odd, z, x), jnp.where(odd, x, z)], axis=-1)

    both = lambda x: jnp.concatenate([x, x], axis=-1)
    flat = lambda x: x.reshape(nlay, x.shape[1], ng * npst)
    pw_re, pw_im = apow(jnp.arange(lc + 1))
    ac_re, ac_im = apow(jnp.array([lc]))
    as_re, as_im = apow(jnp.array([lc * nl]))
    return {
        "pw_re": both(pw_re), "pw_im": both(pw_im),
        "bb_re": half(bb_re), "bb_im": half(bb_im), "c_re": half(c_re), "c_im": half(c_im),
        "ac_re": flat(ac_re), "ac_im": flat(ac_im), "as_re": flat(as_re), "as_im": flat(as_im),
    }


def _outproj_body(m_ref, y_ref, u_ref, d_ref, wg_ref, bg_ref, wm_ref, ws_ref, x_ref, g_ref, xo_ref, ho_ref, *, ts):
    for r0 in range(0, m_ref.shape[0], ts):
        rows = slice(r0, r0 + ts)
        z = jax.nn.gelu(y_ref[rows, :] + d_ref[...] * u_ref[rows, :])
        gate = jnp.dot(z.astype(BF16), wg_ref[...], preferred_element_type=F32) + bg_ref[...]
        s_out = (z * jax.nn.sigmoid(gate)).astype(BF16)
        acc = jnp.dot(m_ref[rows, :], wm_ref[...], preferred_element_type=F32)
        acc = acc + jnp.dot(s_out, ws_ref[...], preferred_element_type=F32)
        x = x_ref[rows, :] + acc
        xo_ref[rows, :] = x
        ms = jnp.mean(x * x, axis=-1, keepdims=True)
        ho_ref[rows, :] = (x * lax.rsqrt(ms + EPS) * g_ref[...]).astype(ho_ref.dtype)


def _outproj(m_out, y, proj, s5_d, w_glu, b_glu, w, l, x, g, tm=512, ts=256):
    m, d = x.shape
    kh = m_out.shape[1]
    u_blk = PROJ_U_COL // kh
    const = lambda shape: pl.BlockSpec(shape, lambda i: (0,) * len(shape))
    resident = dict(pipeline_mode=pl.Buffered(1))
    return pl.pallas_call(
        functools.partial(_outproj_body, ts=ts),
        grid=(m // tm,),
        in_specs=[
            pl.BlockSpec((tm, kh), lambda i: (i, 0)),
            pl.BlockSpec((tm, kh), lambda i: (i, 0)),
            pl.BlockSpec((tm, kh), lambda i: (i, u_blk)),
            const((1, kh)),
            pl.BlockSpec((None, kh, kh), lambda i: (l, 0, 0), **resident),
            const((1, kh)),
            pl.BlockSpec((None, kh, d), lambda i: (l, 0, 0), **resident),
            pl.BlockSpec((None, kh, d), lambda i: (l, 1, 0), **resident),
            pl.BlockSpec((tm, d), lambda i: (i, 0)),
            const((1, d)),
        ],
        out_specs=[pl.BlockSpec((tm, d), lambda i: (i, 0)), pl.BlockSpec((tm, d), lambda i: (i, 0))],
        out_shape=[jax.ShapeDtypeStruct((m, d), F32), jax.ShapeDtypeStruct((m, d), BF16)],
        compiler_params=_cparams("parallel"),
        name="out_proj",
    )(m_out, y, proj, s5_d, w_glu, b_glu, w, w, x, g)


def _ffn_up_body(h_ref, wg_ref, wv_ref, cw_ref, cb_ref, a_ref, wb_ref, ext_ref, *, tm, ts, tiles_per_seq):
    i = pl.program_id(1)
    tn = a_ref.shape[1]

    @pl.when(i == 0)
    def _():
        wb_ref[:, :tn] = wg_ref[...].astype(BF16)
        wb_ref[:, tn:] = wv_ref[...].astype(BF16)

    @pl.when(i % tiles_per_seq == 0)
    def _():
        ext_ref[0:SUBLANES, :] = jnp.zeros((SUBLANES, tn), F32)

    for r0 in range(0, tm, ts):
        gv = jnp.dot(h_ref[r0:r0 + ts, :], wb_ref[...], preferred_element_type=F32)
        g = gv[:, :tn]
        ext_ref[SUBLANES + r0:SUBLANES + r0 + ts, :] = g
        conv = cb_ref[...] + cw_ref[F_CONV - 1:F_CONV, :] * g
        for kk in range(F_CONV - 1):
            conv = conv + cw_ref[kk:kk + 1, :] * ext_ref[pl.ds(SUBLANES + r0 - (F_CONV - 1) + kk, ts), :]
        a_ref[r0:r0 + ts, :] = (jax.nn.gelu(conv) * gv[:, tn:]).astype(a_ref.dtype)
    ext_ref[0:SUBLANES, :] = ext_ref[tm:tm + SUBLANES, :]


def _ffn_up(h, wg, wv, cw, cb, l, t, tm=2048, tn=512, ts=1024):
    m, k = h.shape
    n = wg.shape[2]
    tm = min(tm, t)
    ts = min(ts, tm)
    return pl.pallas_call(
        functools.partial(_ffn_up_body, tm=tm, ts=ts, tiles_per_seq=t // tm),
        grid=(pl.cdiv(n, tn), m // tm),
        in_specs=[
            pl.BlockSpec((tm, k), lambda j, i: (i, 0)),
            pl.BlockSpec((None, k, tn), lambda j, i: (l, 0, j)),
            pl.BlockSpec((None, k, tn), lambda j, i: (l, 0, j)),
            pl.BlockSpec((None, F_CONV, tn), lambda j, i: (l, 0, j)),
            pl.BlockSpec((None, 1, tn), lambda j, i: (l, 0, j)),
        ],
        out_specs=pl.BlockSpec((tm, tn), lambda j, i: (i, j)),
        out_shape=jax.ShapeDtypeStruct((m, n), BF16),
        scratch_shapes=[
            pltpu.VMEM((k, 2 * tn), BF16),
            pltpu.VMEM((tm + SUBLANES, tn), F32),
        ],
        compiler_params=_cparams("parallel", "arbitrary"),
        name="ffn_up",
    )(h, wg, wv, cw, cb)


def _ffn_down_body(a_ref, w_ref, x_ref, xo_ref, wb_ref):
    @pl.when(pl.program_id(1) == 0)
    def _():
        wb_ref[...] = w_ref[...].astype(BF16)

    xo_ref[...] = x_ref[...] + jnp.dot(a_ref[...], wb_ref[...], preferred_element_type=F32)


def _ffn_down(a, w, l, x, tm=1024, tn=512):
    m, d = x.shape
    k = a.shape[1]
    vmem_need = 2 * tm * k * 2 + 2 * k * tn * 4 + k * tn * 2 + 4 * tm * tn * 4
    return pl.pallas_call(
        _ffn_down_body,
        grid=(d // tn, m // tm),
        in_specs=[
            pl.BlockSpec((tm, k), lambda j, i: (i, 0)),
            pl.BlockSpec((None, k, tn), lambda j, i: (l, 0, j)),
            pl.BlockSpec((tm, tn), lambda j, i: (i, j)),
        ],
        out_specs=pl.BlockSpec((tm, tn), lambda j, i: (i, j)),
        out_shape=jax.ShapeDtypeStruct((m, d), F32),
        scratch_shapes=[pltpu.VMEM((k, tn), BF16)],
        compiler_params=_cparams("parallel", "arbitrary", vmem_limit=max(VMEM_LIMIT, vmem_need + VMEM_SLACK)),
        name="ffn_down",
    )(a, w, x)


def _pad_cols(w, mult):
    pad = (-w.shape[-1]) % mult
    return jnp.pad(w, [(0, 0)] * (w.ndim - 1) + [(0, pad)])


def _trunk(x, norm_mix_g, w_in, m_conv_w, m_conv_b, w_q, w_k, m_i_bias, m_f_bias, m_head_g, m_skip,
           s5_a_re, s5_a_im, s5_log_dt, s5_b_re, s5_b_im, s5_c_re, s5_c_im, s5_d, s5_w_glu, s5_b_glu,
           w_out, norm_ffn_g, w_gate, w_val, f_conv_w, f_conv_b, w_down, norm_final_g):
    bsz, t, d = x.shape
    depth = w_in.shape[0]
    bt = bsz * t
    nl = t // (SUBLANES // bsz) // S5_CHUNK

    c3 = PROJ_U_COL
    w_tail = jnp.concatenate([
        w_in[:, :, c3 + 2 * M_HEADS:],
        _pad_cols(w_in[:, :, c3:c3 + M_HEADS], LANES), _pad_cols(w_in[:, :, c3 + M_HEADS:c3 + 2 * M_HEADS], LANES),
    ], axis=-1).astype(BF16)
    w_head = w_in.astype(BF16)
    ib = _pad_cols(m_i_bias, LANES)[:, None, :]
    fb = _pad_cols(m_f_bias, LANES)[:, None, :]
    wq_b, wk_b = w_q.astype(BF16), w_k.astype(BF16)
    s5p = _s5_params(s5_a_re, s5_a_im, s5_log_dt, s5_b_re, s5_b_im, s5_c_re, s5_c_im, nl)
    wglu_b = s5_w_glu.astype(BF16)
    wout_b = w_out.astype(BF16)

    xf = x.reshape(bt, d)
    for l in range(depth):
        proj = _in_proj(xf, norm_mix_g[l][None], w_head, w_tail, l)
        m_out = _mlstm(proj, m_conv_w[l], m_conv_b[l][None], wq_b[l], wk_b[l], ib[l], fb[l],
                       m_head_g[l][None], m_skip[l][None], bsz, t)
        y = _s5(proj, {k_: v_[l] for k_, v_ in s5p.items()}, bsz, t)
        xf, h = _outproj(m_out, y, proj, s5_d[l][None], wglu_b, s5_b_glu[l][None], wout_b, l, xf, norm_ffn_g[l][None])
        a = _ffn_up(h, w_gate, w_val, f_conv_w, f_conv_b[:, None, :], l, t)
        xf = _ffn_down(a, w_down, l, xf)
    return _rmsnorm(xf, norm_final_g, F32).reshape(bsz, t, d)


def kernel(x, norm_mix_g, w_in, m_conv_w, m_conv_b, w_q, w_k, m_i_bias, m_f_bias, m_head_g, m_skip, s5_a_re, s5_a_im, s5_log_dt, s5_b_re, s5_b_im, s5_c_re, s5_c_im, s5_d, s5_w_glu, s5_b_glu, w_out, norm_ffn_g, w_gate, w_val, f_conv_w, f_conv_b, w_down, norm_final_g):
    return _trunk(x, norm_mix_g, w_in, m_conv_w, m_conv_b, w_q, w_k, m_i_bias, m_f_bias, m_head_g, m_skip,
                  s5_a_re, s5_a_im, s5_log_dt, s5_b_re, s5_b_im, s5_c_re, s5_c_im, s5_d, s5_w_glu, s5_b_glu,
                  w_out, norm_ffn_g, w_gate, w_val, f_conv_w, f_conv_b, w_down, norm_final_g)
```

```python
import functools

import jax
import jax.numpy as jnp
from jax import lax
from jax.experimental import pallas as pl
from jax.experimental.pallas import tpu as pltpu

F32 = jnp.float32
BF16 = jnp.bfloat16
EPS = 1e-6

M_HEADS = 4
M_HEAD_V = 256
M_HEAD_QK = 128
M_WIDTH = M_HEADS * M_HEAD_V
M_CONV = 4
M_CHUNK = 128
S5_GROUP_CH = 16
S5_STATE = 64
S5_CH = 1024
S5_CHUNK = 8
F_CONV = 3

LANES = 128
SUBLANES = 8
PROJ_U_COL = 3 * M_WIDTH
PROJ_GATE_COL = PROJ_U_COL + S5_CH
PROJ_COLS = PROJ_GATE_COL + 2 * LANES
S5_SLAB_GROUPS = LANES // S5_GROUP_CH
S5_SLAB_STATE = S5_SLAB_GROUPS * S5_STATE
VMEM_LIMIT = 56 * 1024 * 1024
VMEM_SLACK = 1024 * 1024


def _cparams(*sem, vmem_limit=VMEM_LIMIT):
    return pltpu.CompilerParams(dimension_semantics=sem, vmem_limit_bytes=vmem_limit)


def _rmsnorm_body(x_ref, g_ref, o_ref):
    x = x_ref[...]
    ms = jnp.mean(x * x, axis=-1, keepdims=True)
    o_ref[...] = (x * lax.rsqrt(ms + EPS) * g_ref[...]).astype(o_ref.dtype)


def _rmsnorm(x, g, out_dtype, tm=512):
    m, d = x.shape
    return pl.pallas_call(
        _rmsnorm_body,
        grid=(m // tm,),
        in_specs=[pl.BlockSpec((tm, d), lambda i: (i, 0)), pl.BlockSpec((1, d), lambda i: (0, 0))],
        out_specs=pl.BlockSpec((tm, d), lambda i: (i, 0)),
        out_shape=jax.ShapeDtypeStruct((m, d), out_dtype),
        compiler_params=_cparams("parallel"),
        name="rmsnorm",
    )(x, g.reshape(1, d))


def _in_proj_body(x_ref, g_ref, wh_ref, wt_ref, o_ref, *, ts):
    n_head = wh_ref.shape[1]
    for r0 in range(0, x_ref.shape[0], ts):
        x = x_ref[r0:r0 + ts, :]
        ms = jnp.mean(x * x, axis=-1, keepdims=True)
        h = (x * lax.rsqrt(ms + EPS) * g_ref[...]).astype(BF16)
        o_ref[r0:r0 + ts, :n_head] = jnp.dot(h, wh_ref[...], preferred_element_type=F32)
        o_ref[r0:r0 + ts, n_head:] = jnp.dot(h, wt_ref[...], preferred_element_type=F32)


def _in_proj(x, g, w_head, w_tail, l, tm=512, ts=256):
    m, k = x.shape
    n_head, n_tail = PROJ_U_COL, w_tail.shape[2]
    resident = dict(pipeline_mode=pl.Buffered(1))
    return pl.pallas_call(
        functools.partial(_in_proj_body, ts=ts),
        grid=(m // tm,),
        in_specs=[
            pl.BlockSpec((tm, k), lambda i: (i, 0)),
            pl.BlockSpec((1, k), lambda i: (0, 0)),
            pl.BlockSpec((None, k, n_head), lambda i: (l, 0, 0), **resident),
            pl.BlockSpec((None, k, n_tail), lambda i: (l, 0, 0), **resident),
        ],
        out_specs=pl.BlockSpec((tm, n_head + n_tail), lambda i: (i, 0)),
        out_shape=jax.ShapeDtypeStruct((m, n_head + n_tail), F32),
        compiler_params=_cparams("parallel"),
        name="in_proj",
    )(x, g, w_head, w_tail)


def _log_sigmoid(x):
    return jnp.minimum(x, 0.0) - jnp.log1p(jnp.exp(-jnp.abs(x)))


def _mlstm_body(xm_ref, v_ref, o_ref, ig_ref, fg_ref, cw_ref, cb_ref, wq_ref, wk_ref, ib_ref, fb_ref,
                hg_ref, sk_ref, out_ref, ext_ref, c_ref, n_ref, m_ref, h_ref, *, tb):
    ncb = tb // M_CHUNK
    L = M_CHUNK

    @pl.when(pl.program_id(1) == 0)
    def _():
        ext_ref[0:SUBLANES, :] = jnp.zeros((SUBLANES, M_WIDTH), F32)
        c_ref[...] = jnp.zeros_like(c_ref)
        n_ref[...] = jnp.zeros_like(n_ref)
        m_ref[...] = jnp.zeros_like(m_ref)

    ext_ref[SUBLANES:, :] = xm_ref[...]
    acc = cb_ref[...] + cw_ref[0:1, :] * ext_ref[pl.ds(SUBLANES - 3, tb), :]
    for kk in range(1, M_CONV):
        acc = acc + cw_ref[kk:kk + 1, :] * ext_ref[pl.ds(SUBLANES - 3 + kk, tb), :]
    c = acc * jax.nn.sigmoid(acc)
    ext_ref[0:SUBLANES, :] = xm_ref[tb - SUBLANES:tb, :]

    i_pre = ig_ref[...] + ib_ref[...]
    lf = _log_sigmoid(fg_ref[...] + fb_ref[...])
    pos = lax.broadcasted_iota(jnp.int32, (tb, LANES), 0) % L
    bc = lf
    s = 1
    while s < L:
        bc = bc + jnp.where(pos >= s, pltpu.roll(bc, s, axis=0), 0.0)
        s *= 2
    rowv = i_pre - bc
    a = rowv
    s = 1
    while s < L:
        a = jnp.where(pos >= s, jnp.maximum(a, pltpu.roll(a, s, axis=0)), a)
        s *= 2
    bc3 = bc.reshape(ncb, L, LANES)
    g = bc3[:, L - 1:L, :]
    dec3 = g - bc3 + i_pre.reshape(ncb, L, LANES)
    maxdec = jnp.max(dec3, axis=1, keepdims=True)
    m_run = m_ref[0:1, :]
    m0_l, m1_l = [], []
    for nn in range(ncb):
        m0_l.append(m_run)
        m_run = jnp.maximum(g[nn] + m_run, maxdec[nn])
        m1_l.append(m_run)
    m_ref[0:1, :] = m_run
    m0 = jnp.stack(m0_l, axis=0)
    m1 = jnp.stack(m1_l, axis=0)
    mx = jnp.maximum(m0, a.reshape(ncb, L, LANES))
    colv = -mx
    w_inter = jnp.exp(m0 - mx)
    e_negm = jnp.exp(-(bc3 + mx))
    w_k = jnp.exp(dec3 - m1)
    w_c = jnp.exp(g + m0 - m1)
    rowv_t = rowv.T

    causal = lax.broadcasted_iota(jnp.int32, (L, L), 1) <= lax.broadcasted_iota(jnp.int32, (L, L), 0)
    scale = M_HEAD_QK ** -0.5

    for h in range(M_HEADS):
        c_h = c[:, h * M_HEAD_V:(h + 1) * M_HEAD_V].astype(BF16)
        q = jnp.dot(c_h, wq_ref[h], preferred_element_type=F32) * scale
        k = jnp.dot(c_h, wk_ref[h], preferred_element_type=F32)
        qb = q.astype(BF16)
        kb = k.astype(BF16)
        vb = v_ref[:, h * M_HEAD_V:(h + 1) * M_HEAD_V].astype(BF16)
        c_st = c_ref[h]
        n_st = n_ref[h]
        for nn in range(ncb):
            r0 = nn * L
            q_n = q[r0:r0 + L]
            qb_n = qb[r0:r0 + L]
            sc = lax.dot_general(qb_n, kb[r0:r0 + L], (((1,), (1,)), ((), ())),
                                 preferred_element_type=F32)
            arg = colv[nn][:, h:h + 1] + rowv_t[h:h + 1, r0:r0 + L]
            sw = sc * jnp.exp(jnp.where(causal, arg, -jnp.inf))
            wi = w_inter[nn][:, h:h + 1]
            num = jnp.dot(sw.astype(BF16), vb[r0:r0 + L], preferred_element_type=F32)
            num = num + wi * jnp.dot(qb_n, c_st.astype(BF16), preferred_element_type=F32)
            den = jnp.sum(sw, axis=1, keepdims=True) + wi * jnp.sum(q_n * n_st, axis=1, keepdims=True)
            den = jnp.maximum(jnp.abs(den), e_negm[nn][:, h:h + 1])
            h_ref[r0:r0 + L, h * M_HEAD_V:(h + 1) * M_HEAD_V] = num / den
            kw = k[r0:r0 + L] * w_k[nn][:, h:h + 1]
            wc = w_c[nn][:, h:h + 1]
            upd = lax.dot_general(kw.astype(BF16), vb[r0:r0 + L], (((0,), (0,)), ((), ())),
                                  preferred_element_type=F32)
            c_st = wc * c_st + upd
            n_st = wc * n_st + jnp.sum(kw, axis=0, keepdims=True)
        c_ref[h] = c_st
        n_ref[h] = n_st

    for h in range(M_HEADS):
        sl = slice(h * M_HEAD_V, (h + 1) * M_HEAD_V)
        hh = h_ref[:, sl]
        hn = hh * lax.rsqrt(jnp.mean(hh * hh, axis=-1, keepdims=True) + EPS) * hg_ref[:, sl]
        out = jax.nn.sigmoid(o_ref[:, sl]) * (hn + sk_ref[:, sl] * c[:, sl])
        out_ref[:, sl] = out.astype(out_ref.dtype)


def _mlstm(proj, cw, cb, wq, wk, ib, fb, hg, sk, bsz, t, tb=512):
    nt = t // tb
    gate_blk = PROJ_GATE_COL // LANES
    row = lambda b, i: b * nt + i
    full = lambda shape: pl.BlockSpec(shape, lambda b, i: (0,) * len(shape))
    return pl.pallas_call(
        functools.partial(_mlstm_body, tb=tb),
        grid=(bsz, nt),
        in_specs=[
            pl.BlockSpec((tb, M_WIDTH), lambda b, i: (row(b, i), 0)),
            pl.BlockSpec((tb, M_WIDTH), lambda b, i: (row(b, i), 1)),
            pl.BlockSpec((tb, M_WIDTH), lambda b, i: (row(b, i), 2)),
            pl.BlockSpec((tb, LANES), lambda b, i: (row(b, i), gate_blk)),
            pl.BlockSpec((tb, LANES), lambda b, i: (row(b, i), gate_blk + 1)),
            full((M_CONV, M_WIDTH)), full((1, M_WIDTH)),
            full((M_HEADS, M_HEAD_V, M_HEAD_QK)), full((M_HEADS, M_HEAD_V, M_HEAD_QK)),
            full((1, LANES)), full((1, LANES)), full((1, M_WIDTH)), full((1, M_WIDTH)),
        ],
        out_specs=pl.BlockSpec((tb, M_WIDTH), lambda b, i: (row(b, i), 0)),
        out_shape=jax.ShapeDtypeStruct((bsz * t, M_WIDTH), BF16),
        scratch_shapes=[
            pltpu.VMEM((tb + SUBLANES, M_WIDTH), F32),
            pltpu.VMEM((M_HEADS, M_HEAD_QK, M_HEAD_V), F32),
            pltpu.VMEM((M_HEADS, 1, M_HEAD_QK), F32),
            pltpu.VMEM((SUBLANES, LANES), F32),
            pltpu.VMEM((tb, M_WIDTH), F32),
        ],
        compiler_params=_cparams("parallel", "arbitrary"),
        name="mlstm",
    )(proj, proj, proj, proj, proj, cw, cb, wq, wk, ib, fb, hg, sk)


def _s5_body(u_ref, pre_ref, pim_ref, bre_ref, bim_ref, cre_ref, cim_ref, ar_ref, ai_ref, br_ref, bi_ref,
             y_ref, wst_ref, wct_ref, bbr_ref, toe_ref, v_ref, s_ref, yall_ref,
             *, nl, nsg, tseg):
    lc = S5_CHUNK
    ns = S5_SLAB_STATE
    gch = S5_GROUP_CH
    nseq = SUBLANES

    @pl.when(pl.program_id(0) == 0)
    def _():
        wst_ref[...] = jnp.zeros_like(wst_ref)
        wct_ref[...] = jnp.zeros_like(wct_ref)
        bbr_ref[...] = jnp.zeros_like(bbr_ref)
        toe_ref[...] = jnp.zeros_like(toe_ref)

    for gg in range(S5_SLAB_GROUPS):
        k0 = (gg // 2) * LANES
        bre, bim, cre, cim = bre_ref[gg], bim_ref[gg], cre_ref[gg], cim_ref[gg]
        r0 = gg * gch
        bbr_ref[r0:r0 + gch, k0:k0 + LANES] = bre.astype(BF16)
        bbr_ref[r0:r0 + gch, ns + k0:ns + k0 + LANES] = bim.astype(BF16)
        for j in range(lc + 1):
            p_re = pre_ref[j, gg:gg + 1, :]
            p_im = pim_ref[j, gg:gg + 1, :]
            r1 = j * LANES + r0
            wct_ref[r1:r1 + gch, k0:k0 + LANES] = (cre * p_re - cim * p_im).astype(BF16)
            wct_ref[r1:r1 + gch, ns + k0:ns + k0 + LANES] = (-(cre * p_im + cim * p_re)).astype(BF16)
            if j < lc:
                r2 = (lc - 1 - j) * LANES + r0
                wst_ref[r2:r2 + gch, k0:k0 + LANES] = (bre * p_re - bim * p_im).astype(BF16)
                wst_ref[r2:r2 + gch, ns + k0:ns + k0 + LANES] = (bre * p_im + bim * p_re).astype(BF16)
    kt = lax.dot_general(bbr_ref[...], wct_ref[0:lc * LANES, :], (((1,), (1,)), ((), ())),
                         preferred_element_type=F32).astype(BF16)
    for i in range(lc):
        toe_ref[i * LANES:(i + 1) * LANES, i * LANES:] = kt[:, :(lc - i) * LANES]

    ut = jnp.swapaxes(u_ref[...].reshape(nseq, tseg, LANES), 0, 1).reshape(nl, lc, nseq, LANES)
    uc = jnp.concatenate([ut[:, i] for i in range(lc)], axis=-1).reshape(nl * nseq, lc * LANES).astype(BF16)
    yall_ref[...] = jnp.dot(uc, toe_ref[...], preferred_element_type=F32)
    v_ref[...] = jnp.dot(uc, wst_ref[...], preferred_element_type=F32)

    ar = jnp.broadcast_to(ar_ref[...], (nseq, ns))
    ai = jnp.broadcast_to(ai_ref[...], (nseq, ns))

    def scan_step(n, carry):
        s_re, s_im = carry
        r0 = pl.multiple_of(n * nseq, nseq)
        s_ref[pl.ds(r0, nseq), 0:ns] = s_re
        s_ref[pl.ds(r0, nseq), ns:] = s_im
        n_re = ar * s_re - ai * s_im + v_ref[pl.ds(r0, nseq), 0:ns]
        n_im = ar * s_im + ai * s_re + v_ref[pl.ds(r0, nseq), ns:]
        return n_re, n_im

    zero = jnp.zeros((nseq, ns), F32)
    f_re, f_im = lax.fori_loop(0, nl, scan_step, (zero, zero))

    br = jnp.broadcast_to(br_ref[...], (nseq, ns))
    bi = jnp.broadcast_to(bi_ref[...], (nseq, ns))
    has_prev = lax.broadcasted_iota(jnp.int32, (nseq, ns), 0) % nsg >= 1
    i_re, i_im = zero, zero
    for _ in range(nsg - 1):
        t_re = f_re + br * i_re - bi * i_im
        t_im = f_im + br * i_im + bi * i_re
        i_re = jnp.where(has_prev, pltpu.roll(t_re, 1, axis=0), 0.0)
        i_im = jnp.where(has_prev, pltpu.roll(t_im, 1, axis=0), 0.0)

    def fix_step(n, carry):
        c_re, c_im = carry
        r0 = pl.multiple_of(n * nseq, nseq)
        s_ref[pl.ds(r0, nseq), 0:ns] += c_re
        s_ref[pl.ds(r0, nseq), ns:] += c_im
        return ar * c_re - ai * c_im, ar * c_im + ai * c_re

    lax.fori_loop(0, nl, fix_step, (i_re, i_im))

    ya = yall_ref[...] + lax.dot_general(s_ref[...].astype(BF16), wct_ref[LANES:(lc + 1) * LANES, :],
                                         (((1,), (1,)), ((), ())), preferred_element_type=F32)
    ya = ya.reshape(nl, nseq, lc * LANES)
    yt = jnp.stack([ya[:, :, j * LANES:(j + 1) * LANES] for j in range(lc)], axis=1)
    y_ref[...] = jnp.swapaxes(yt.reshape(tseg, nseq, LANES), 0, 1).reshape(nseq * tseg, LANES)


def _s5(proj, prm, bsz, t):
    bt = bsz * t
    nsg = SUBLANES // bsz
    tseg = t // nsg
    nl = tseg // S5_CHUNK
    r = nl * SUBLANES
    lc = S5_CHUNK
    ns = S5_SLAB_STATE
    n_slab = prm["bb_re"].shape[0] // S5_SLAB_GROUPS
    u_blk0 = PROJ_U_COL // LANES
    grp = lambda shape: pl.BlockSpec((S5_SLAB_GROUPS,) + shape, lambda s: (s,) + (0,) * len(shape))
    vec = lambda rows: pl.BlockSpec((rows, ns), lambda s: (0, s))
    return pl.pallas_call(
        functools.partial(_s5_body, nl=nl, nsg=nsg, tseg=tseg),
        grid=(n_slab,),
        in_specs=[
            pl.BlockSpec((bt, LANES), lambda s: (0, u_blk0 + s)),
            pl.BlockSpec((lc + 1, S5_SLAB_GROUPS, LANES), lambda s: (0, s, 0)),
            pl.BlockSpec((lc + 1, S5_SLAB_GROUPS, LANES), lambda s: (0, s, 0)),
            grp((S5_GROUP_CH, LANES)), grp((S5_GROUP_CH, LANES)), grp((S5_GROUP_CH, LANES)), grp((S5_GROUP_CH, LANES)),
            vec(1), vec(1), vec(1), vec(1),
        ],
        out_specs=pl.BlockSpec((bt, LANES), lambda s: (0, s)),
        out_shape=jax.ShapeDtypeStruct((bt, n_slab * LANES), F32),
        scratch_shapes=[
            pltpu.VMEM((lc * LANES, 2 * ns), BF16),
            pltpu.VMEM(((lc + 1) * LANES, 2 * ns), BF16),
            pltpu.VMEM((LANES, 2 * ns), BF16),
            pltpu.VMEM((lc * LANES, lc * LANES), BF16),
            pltpu.VMEM((r, 2 * ns), F32),
            pltpu.VMEM((r, 2 * ns), F32),
            pltpu.VMEM((r, lc * LANES), F32),
        ],
        compiler_params=_cparams("arbitrary"),
        name="s5",
    )(proj, prm["pw_re"], prm["pw_im"], prm["bb_re"], prm["bb_im"], prm["c_re"], prm["c_im"],
      prm["ac_re"], prm["ac_im"], prm["as_re"], prm["as_im"])


def _s5_params(a_re, a_im, log_dt, b_re, b_im, c_re, c_im, nl):
    lc = S5_CHUNK
    nlay, ng, npst = a_re.shape
    dt = jnp.exp(log_dt)[..., None]
    lam_re, lam_im = a_re * dt, a_im * dt

    def apow(n):
        n = n.astype(F32)[None, :, None, None]
        mag = jnp.exp(lam_re[:, None] * n)
        return mag * jnp.cos(lam_im[:, None] * n), mag * jnp.sin(lam_im[:, None] * n)

    ab_re, ab_im = jnp.exp(lam_re) * jnp.cos(lam_im), jnp.exp(lam_re) * jnp.sin(lam_im)
    den = a_re * a_re + a_im * a_im
    f_re = ((ab_re - 1.0) * a_re + ab_im * a_im) / den
    f_im = (ab_im * a_re - (ab_re - 1.0) * a_im) / den
    bb_re = (f_re[..., None] * b_re - f_im[..., None] * b_im).transpose(0, 1, 3, 2)
    bb_im = (f_re[..., None] * b_im + f_im[..., None] * b_re).transpose(0, 1, 3, 2)
    odd = (jnp.arange(ng) % 2 == 1)[None, :, None, None]

    def half(x):
        z = jnp.zeros_like(x)
        return jnp.concatenate([jnp.where(odd, z, x), jnp.where(odd, x, z)], axis=-1)

    both = lambda x: jnp.concatenate([x, x], axis=-1)
    flat = lambda x: x.reshape(nlay, x.shape[1], ng * npst)
    pw_re, pw_im = apow(jnp.arange(lc + 1))
    ac_re, ac_im = apow(jnp.array([lc]))
    as_re, as_im = apow(jnp.array([lc * nl]))
    return {
        "pw_re": both(pw_re), "pw_im": both(pw_im),
        "bb_re": half(bb_re), "bb_im": half(bb_im), "c_re": half(c_re), "c_im": half(c_im),
        "ac_re": flat(ac_re), "ac_im": flat(ac_im), "as_re": flat(as_re), "as_im": flat(as_im),
    }


def _outproj_body(m_ref, y_ref, u_ref, d_ref, wg_ref, bg_ref, wm_ref, ws_ref, x_ref, g_ref, xo_ref, ho_ref, *, ts):
    for r0 in range(0, m_ref.shape[0], ts):
        rows = slice(r0, r0 + ts)
        z = jax.nn.gelu(y_ref[rows, :] + d_ref[...] * u_ref[rows, :])
        gate = jnp.dot(z.astype(BF16), wg_ref[...], preferred_element_type=F32) + bg_ref[...]
        s_out = (z * jax.nn.sigmoid(gate)).astype(BF16)
        acc = jnp.dot(m_ref[rows, :], wm_ref[...], preferred_element_type=F32)
        acc = acc + jnp.dot(s_out, ws_ref[...], preferred_element_type=F32)
        x = x_ref[rows, :] + acc
        xo_ref[rows, :] = x
        ms = jnp.mean(x * x, axis=-1, keepdims=True)
        ho_ref[rows, :] = (x * lax.rsqrt(ms + EPS) * g_ref[...]).astype(ho_ref.dtype)


def _outproj(m_out, y, proj, s5_d, w_glu, b_glu, w, l, x, g, tm=512, ts=256):
    m, d = x.shape
    kh = m_out.shape[1]
    u_blk = PROJ_U_COL // kh
    const = lambda shape: pl.BlockSpec(shape, lambda i: (0,) * len(shape))
    resident = dict(pipeline_mode=pl.Buffered(1))
    return pl.pallas_call(
        functools.partial(_outproj_body, ts=ts),
        grid=(m // tm,),
        in_specs=[
            pl.BlockSpec((tm, kh), lambda i: (i, 0)),
            pl.BlockSpec((tm, kh), lambda i: (i, 0)),
            pl.BlockSpec((tm, kh), lambda i: (i, u_blk)),
            const((1, kh)),
            pl.BlockSpec((None, kh, kh), lambda i: (l, 0, 0), **resident),
            const((1, kh)),
            pl.BlockSpec((None, kh, d), lambda i: (l, 0, 0), **resident),
            pl.BlockSpec((None, kh, d), lambda i: (l, 1, 0), **resident),
            pl.BlockSpec((tm, d), lambda i: (i, 0)),
            const((1, d)),
        ],
        out_specs=[pl.BlockSpec((tm, d), lambda i: (i, 0)), pl.BlockSpec((tm, d), lambda i: (i, 0))],
        out_shape=[jax.ShapeDtypeStruct((m, d), F32), jax.ShapeDtypeStruct((m, d), BF16)],
        compiler_params=_cparams("parallel"),
        name="out_proj",
    )(m_out, y, proj, s5_d, w_glu, b_glu, w, w, x, g)


def _ffn_up_body(h_ref, wg_ref, wv_ref, cw_ref, cb_ref, a_ref, wb_ref, ext_ref, *, tm, ts, tiles_per_seq):
    i = pl.program_id(1)
    tn = a_ref.shape[1]

    @pl.when(i == 0)
    def _():
        wb_ref[:, :tn] = wg_ref[...].astype(BF16)
        wb_ref[:, tn:] = wv_ref[...].astype(BF16)

    @pl.when(i % tiles_per_seq == 0)
    def _():
        ext_ref[0:SUBLANES, :] = jnp.zeros((SUBLANES, tn), F32)

    for r0 in range(0, tm, ts):
        gv = jnp.dot(h_ref[r0:r0 + ts, :], wb_ref[...], preferred_element_type=F32)
        g = gv[:, :tn]
        ext_ref[SUBLANES + r0:SUBLANES + r0 + ts, :] = g
        conv = cb_ref[...] + cw_ref[F_CONV - 1:F_CONV, :] * g
        for kk in range(F_CONV - 1):
            conv = conv + cw_ref[kk:kk + 1, :] * ext_ref[pl.ds(SUBLANES + r0 - (F_CONV - 1) + kk, ts), :]
        a_ref[r0:r0 + ts, :] = (jax.nn.gelu(conv) * gv[:, tn:]).astype(a_ref.dtype)
    ext_ref[0:SUBLANES, :] = ext_ref[tm:tm + SUBLANES, :]


def _ffn_up(h, wg, wv, cw, cb, l, t, tm=2048, tn=512, ts=1024):
    m, k = h.shape
    n = wg.shape[2]
    tm = min(tm, t)
    ts = min(ts, tm)
    return pl.pallas_call(
        functools.partial(_ffn_up_body, tm=tm, ts=ts, tiles_per_seq=t // tm),
        grid=(pl.cdiv(n, tn), m // tm),
        in_specs=[
            pl.BlockSpec((tm, k), lambda j, i: (i, 0)),
            pl.BlockSpec((None, k, tn), lambda j, i: (l, 0, j)),
            pl.BlockSpec((None, k, tn), lambda j, i: (l, 0, j)),
            pl.BlockSpec((None, F_CONV, tn), lambda j, i: (l, 0, j)),
            pl.BlockSpec((None, 1, tn), lambda j, i: (l, 0, j)),
        ],
        out_specs=pl.BlockSpec((tm, tn), lambda j, i: (i, j)),
        out_shape=jax.ShapeDtypeStruct((m, n), BF16),
        scratch_shapes=[
            pltpu.VMEM((k, 2 * tn), BF16),
            pltpu.VMEM((tm + SUBLANES, tn), F32),
        ],
        compiler_params=_cparams("parallel", "arbitrary"),
        name="ffn_up",
    )(h, wg, wv, cw, cb)


def _ffn_down_body(a_ref, w_ref, x_ref, xo_ref, wb_ref):
    @pl.when(pl.program_id(1) == 0)
    def _():
        wb_ref[...] = w_ref[...].astype(BF16)

    xo_ref[...] = x_ref[...] + jnp.dot(a_ref[...], wb_ref[...], preferred_element_type=F32)


def _ffn_down(a, w, l, x, tm=1024, tn=512):
    m, d = x.shape
    k = a.shape[1]
    vmem_need = 2 * tm * k * 2 + 2 * k * tn * 4 + k * tn * 2 + 4 * tm * tn * 4
    return pl.pallas_call(
        _ffn_down_body,
        grid=(d // tn, m // tm),
        in_specs=[
            pl.BlockSpec((tm, k), lambda j, i: (i, 0)),
            pl.BlockSpec((None, k, tn), lambda j, i: (l, 0, j)),
            pl.BlockSpec((tm, tn), lambda j, i: (i, j)),
        ],
        out_specs=pl.BlockSpec((tm, tn), lambda j, i: (i, j)),
        out_shape=jax.ShapeDtypeStruct((m, d), F32),
        scratch_shapes=[pltpu.VMEM((k, tn), BF16)],
        compiler_params=_cparams("parallel", "arbitrary", vmem_limit=max(VMEM_LIMIT, vmem_need + VMEM_SLACK)),
        name="ffn_down",
    )(a, w, x)


def _w_in_prep_body(w_ref, head_ref, tail_ref):
    w = w_ref[...]
    c3 = PROJ_U_COL
    head_ref[...] = w[:, :c3].astype(BF16)
    tail_ref[:, :S5_CH] = w[:, c3 + 2 * M_HEADS:].astype(BF16)
    gates = w[:, c3:c3 + LANES]
    is_gate = lax.broadcasted_iota(jnp.int32, gates.shape, 1) < M_HEADS
    tail_ref[:, S5_CH:S5_CH + LANES] = jnp.where(is_gate, gates, 0.0).astype(BF16)
    f_gates = pltpu.roll(gates, LANES - M_HEADS, axis=1)
    tail_ref[:, S5_CH + LANES:] = jnp.where(is_gate, f_gates, 0.0).astype(BF16)


def _w_in_prep(w_in, tr=256):
    nlay, k, n = w_in.shape
    n_tail = PROJ_COLS - PROJ_U_COL
    return pl.pallas_call(
        _w_in_prep_body,
        grid=(nlay, k // tr),
        in_specs=[pl.BlockSpec((None, tr, n), lambda l, i: (l, i, 0))],
        out_specs=[pl.BlockSpec((None, tr, PROJ_U_COL), lambda l, i: (l, i, 0)),
                   pl.BlockSpec((None, tr, n_tail), lambda l, i: (l, i, 0))],
        out_shape=[jax.ShapeDtypeStruct((nlay, k, PROJ_U_COL), BF16), jax.ShapeDtypeStruct((nlay, k, n_tail), BF16)],
        compiler_params=_cparams("parallel", "parallel"),
        name="w_in_prep",
    )(w_in)


def _cast_body(w_ref, o_ref):
    o_ref[...] = w_ref[...].astype(o_ref.dtype)


def _to_bf16(w, tr=512):
    nlay, r, c = w.shape
    return pl.pallas_call(
        _cast_body,
        grid=(nlay, r // tr),
        in_specs=[pl.BlockSpec((None, tr, c), lambda l, i: (l, i, 0))],
        out_specs=pl.BlockSpec((None, tr, c), lambda l, i: (l, i, 0)),
        out_shape=jax.ShapeDtypeStruct(w.shape, BF16),
        compiler_params=_cparams("parallel", "parallel"),
        name="to_bf16",
    )(w)


def _pad_cols(w, mult):
    pad = (-w.shape[-1]) % mult
    return jnp.pad(w, [(0, 0)] * (w.ndim - 1) + [(0, pad)])


def _trunk(x, norm_mix_g, w_in, m_conv_w, m_conv_b, w_q, w_k, m_i_bias, m_f_bias, m_head_g, m_skip,
           s5_a_re, s5_a_im, s5_log_dt, s5_b_re, s5_b_im, s5_c_re, s5_c_im, s5_d, s5_w_glu, s5_b_glu,
           w_out, norm_ffn_g, w_gate, w_val, f_conv_w, f_conv_b, w_down, norm_final_g):
    bsz, t, d = x.shape
    depth = w_in.shape[0]
    bt = bsz * t
    nl = t // (SUBLANES // bsz) // S5_CHUNK

    w_head, w_tail = _w_in_prep(w_in)
    ib = _pad_cols(m_i_bias, LANES)[:, None, :]
    fb = _pad_cols(m_f_bias, LANES)[:, None, :]
    wq_b, wk_b = w_q.astype(BF16), w_k.astype(BF16)
    s5p = _s5_params(s5_a_re, s5_a_im, s5_log_dt, s5_b_re, s5_b_im, s5_c_re, s5_c_im, nl)
    wglu_b = _to_bf16(s5_w_glu)
    wout_b = _to_bf16(w_out)

    xf = x.reshape(bt, d)
    for l in range(depth):
        proj = _in_proj(xf, norm_mix_g[l][None], w_head, w_tail, l)
        m_out = _mlstm(proj, m_conv_w[l], m_conv_b[l][None], wq_b[l], wk_b[l], ib[l], fb[l],
                       m_head_g[l][None], m_skip[l][None], bsz, t)
        y = _s5(proj, {k_: v_[l] for k_, v_ in s5p.items()}, bsz, t)
        xf, h = _outproj(m_out, y, proj, s5_d[l][None], wglu_b, s5_b_glu[l][None], wout_b, l, xf, norm_ffn_g[l][None])
        a = _ffn_up(h, w_gate, w_val, f_conv_w, f_conv_b[:, None, :], l, t)
        xf = _ffn_down(a, w_down, l, xf)
    return _rmsnorm(xf, norm_final_g, F32).reshape(bsz, t, d)


def kernel(x, norm_mix_g, w_in, m_conv_w, m_conv_b, w_q, w_k, m_i_bias, m_f_bias, m_head_g, m_skip, s5_a_re, s5_a_im, s5_log_dt, s5_b_re, s5_b_im, s5_c_re, s5_c_im, s5_d, s5_w_glu, s5_b_glu, w_out, norm_ffn_g, w_gate, w_val, f_conv_w, f_conv_b, w_down, norm_final_g):
    return _trunk(x, norm_mix_g, w_in, m_conv_w, m_conv_b, w_q, w_k, m_i_bias, m_f_bias, m_head_g, m_skip,
                  s5_a_re, s5_a_im, s5_log_dt, s5_b_re, s5_b_im, s5_c_re, s5_c_im, s5_d, s5_w_glu, s5_b_glu,
                  w_out, norm_ffn_g, w_gate, w_val, f_conv_w, f_conv_b, w_down, norm_final_g)
```
